```python
import jax, jax.numpy as jnp
from jax import lax
import numpy as np

D_MODEL = 4096
BATCH = 4
SEQ = 4096
DEPTH = 1

MIX_WIDTH = D_MODEL
SB_WIDTH = MIX_WIDTH // 2
CONV_WIDTH = MIX_WIDTH - SB_WIDTH
HEAD_DIM = 128
SB_HEADS = SB_WIDTH // HEAD_DIM
CONV_KERNEL = 31
Q_BLOCK = 128
EPS = 1e-6
IN_COLS = 4 * SB_WIDTH + 3 * CONV_WIDTH

kernel_name = "hybrid_stickbreak_conformer_adaln"


def rms_norm(x, g):
    x32 = x.astype(jnp.float32)
    y = x32 * lax.rsqrt(jnp.mean(x32 * x32, axis=-1, keepdims=True) + EPS)
    return (y * g.astype(jnp.float32)).astype(x.dtype)


def layer_norm(x, g, b):
    x32 = x.astype(jnp.float32)
    mu = jnp.mean(x32, axis=-1, keepdims=True)
    xc = x32 - mu
    var = jnp.mean(xc * xc, axis=-1, keepdims=True)
    y = xc * lax.rsqrt(var + EPS) * g.astype(jnp.float32) + b.astype(jnp.float32)
    return y.astype(x.dtype)


def stick_breaking_attention(q, k, v):
    B, H, S, Dh = q.shape
    nb = S // Q_BLOCK
    scale = Dh ** -0.5
    q_blocks = q.reshape(B, H, nb, Q_BLOCK, Dh).transpose(2, 0, 1, 3, 4)
    starts = jnp.arange(nb, dtype=jnp.int32) * Q_BLOCK
    key_pos = jnp.arange(S, dtype=jnp.int32)

    def one_block(args):
        qb, start = args
        z = jnp.einsum('bhqd,bhkd->bhqk', qb, k).astype(jnp.float32) * scale
        q_pos = start + jnp.arange(Q_BLOCK, dtype=jnp.int32)
        causal = key_pos[None, :] < q_pos[:, None]
        log_stay = jnp.where(causal, -jax.nn.softplus(z), 0.0)
        stay_after = lax.cumsum(log_stay, axis=3, reverse=True) - log_stay
        log_w = jax.nn.log_sigmoid(z) + stay_after
        w = jnp.where(causal, jnp.exp(log_w), 0.0)
        return jnp.einsum('bhqk,bhkd->bhqd', w.astype(v.dtype), v)

    out = lax.map(one_block, (q_blocks, starts))
    return out.transpose(1, 2, 0, 3, 4).reshape(B, H, S, Dh)


def conformer_conv(u, g_glu, w_dw, b_dw, ln_g, ln_b, w_pw, b_pw):
    C = u.shape[-1]
    h = u * jax.nn.sigmoid(g_glu)
    h = lax.conv_general_dilated(
        h, w_dw[:, None, :], window_strides=(1,),
        padding=[(CONV_KERNEL - 1, 0)],
        dimension_numbers=('NWC', 'WIO', 'NWC'),
        feature_group_count=C) + b_dw
    h = jax.nn.silu(layer_norm(h, ln_g, ln_b))
    return h @ w_pw + b_pw


def hybrid_layer(x, c, norm_g, w_ada, b_ada, w_in, q_norm_g, k_norm_g,
                 w_dw, b_dw, ln_g, ln_b, w_pw, b_pw, w_out):
    B, S, _ = x.shape
    mod = jax.nn.silu(c) @ w_ada + b_ada
    shift, scale, gate = jnp.split(mod, 3, axis=-1)
    h = rms_norm(x, norm_g) * (1.0 + scale[:, None, :]) + shift[:, None, :]

    proj = h @ w_in
    splits = [SB_WIDTH, 2 * SB_WIDTH, 3 * SB_WIDTH, 4 * SB_WIDTH,
              4 * SB_WIDTH + CONV_WIDTH, 4 * SB_WIDTH + 2 * CONV_WIDTH]
    q, k, v, g_sb, u, g_glu, g_conv = jnp.split(proj, splits, axis=-1)

    def heads(t):
        return t.reshape(B, S, SB_HEADS, HEAD_DIM).transpose(0, 2, 1, 3)

    qh = rms_norm(heads(q), q_norm_g)
    kh = rms_norm(heads(k), k_norm_g)
    o = stick_breaking_attention(qh, kh, heads(v))
    o = o.transpose(0, 2, 1, 3).reshape(B, S, SB_WIDTH)
    y_sb = o * jax.nn.silu(g_sb)

    y_conv = conformer_conv(u, g_glu, w_dw, b_dw, ln_g, ln_b, w_pw, b_pw) * jax.nn.silu(g_conv)

    y = jnp.concatenate([y_sb, y_conv], axis=-1) @ w_out
    return x + gate[:, None, :] * y


def setup_inputs(seed: int = 0) -> dict:
    key = jax.random.key(seed)
    ks = jax.random.split(key, 16)
    f32 = jnp.float32
    D = D_MODEL
    x = jax.random.normal(ks[0], (BATCH, SEQ, D), f32)
    c = jax.random.normal(ks[1], (BATCH, D), f32)
    norm_g = 1.0 + 0.02 * jax.random.normal(ks[2], (DEPTH, D), f32)
    w_ada = 0.5 * D ** -0.5 * jax.random.normal(ks[3], (DEPTH, D, 3 * D), f32)
    b_ada = 0.02 * jax.random.normal(ks[4], (DEPTH, 3 * D), f32)
    w_in = D ** -0.5 * jax.random.normal(ks[5], (DEPTH, D, IN_COLS), f32)
    q_norm_g = 1.0 + 0.02 * jax.random.normal(ks[6], (DEPTH, HEAD_DIM), f32)
    k_norm_g = 1.0 + 0.02 * jax.random.normal(ks[7], (DEPTH, HEAD_DIM), f32)
    w_dw = CONV_KERNEL ** -0.5 * jax.random.normal(ks[8], (DEPTH, CONV_KERNEL, CONV_WIDTH), f32)
    b_dw = 0.02 * jax.random.normal(ks[9], (DEPTH, CONV_WIDTH), f32)
    ln_g = 1.0 + 0.02 * jax.random.normal(ks[10], (DEPTH, CONV_WIDTH), f32)
    ln_b = 0.02 * jax.random.normal(ks[11], (DEPTH, CONV_WIDTH), f32)
    w_pw = CONV_WIDTH ** -0.5 * jax.random.normal(ks[12], (DEPTH, CONV_WIDTH, CONV_WIDTH), f32)
    b_pw = 0.02 * jax.random.normal(ks[13], (DEPTH, CONV_WIDTH), f32)
    w_out = MIX_WIDTH ** -0.5 * jax.random.normal(ks[14], (DEPTH, MIX_WIDTH, D), f32)
    return {"x": x, "c": c, "norm_g": norm_g, "w_ada": w_ada, "b_ada": b_ada,
            "w_in": w_in, "q_norm_g": q_norm_g, "k_norm_g": k_norm_g,
            "w_dw": w_dw, "b_dw": b_dw, "ln_g": ln_g, "ln_b": ln_b,
            "w_pw": w_pw, "b_pw": b_pw, "w_out": w_out}


def reference(x, c, norm_g, w_ada, b_ada, w_in, q_norm_g, k_norm_g,
              w_dw, b_dw, ln_g, ln_b, w_pw, b_pw, w_out):
    for layer in range(DEPTH):
        x = hybrid_layer(x, c, norm_g[layer], w_ada[layer], b_ada[layer], w_in[layer],
                         q_norm_g[layer], k_norm_g[layer], w_dw[layer], b_dw[layer],
                         ln_g[layer], ln_b[layer], w_pw[layer], b_pw[layer], w_out[layer])
    return x
```

```python
import functools
import math

import jax
import jax.numpy as jnp
from jax import lax
from jax.experimental import pallas as pl
from jax.experimental.pallas import tpu as pltpu

F32 = jnp.float32
BF16 = jnp.bfloat16

HEAD_DIM = 128
CONV_KERNEL = 31
EPS = 1e-6
LOG2E = math.log2(math.e)

LANES = 128
SUBLANES = 8
HALO_ROWS = 32
VMEM_LIMIT = 56 * 1024 * 1024

NORM_TM = 512
PROJ_TM, PROJ_TN = 1024, 512
ATTN_T = 256
CONV_TS = 256
CONV_RC = 64
OUT_TM, OUT_TN = 1024, 512
ADA_TN = 512


def _params(*sem):
    return pltpu.CompilerParams(dimension_semantics=sem, vmem_limit_bytes=VMEM_LIMIT)


def _silu(x):
    return x * jax.nn.sigmoid(x)


def _adaln_kernel(c_ref, w_ref, b_ref, o_ref):
    a = _silu(c_ref[...]).astype(BF16)
    o_ref[...] = jnp.dot(a, w_ref[...].astype(BF16), preferred_element_type=F32) + b_ref[...]


def _adaln_mod(c, w_ada, b_ada):
    B, D = c.shape
    N = w_ada.shape[1]
    return pl.pallas_call(
        _adaln_kernel,
        grid=(N // ADA_TN,),
        in_specs=[pl.BlockSpec((B, D), lambda j: (0, 0)),
                  pl.BlockSpec((D, ADA_TN), lambda j: (0, j)),
                  pl.BlockSpec((1, ADA_TN), lambda j: (0, j))],
        out_specs=pl.BlockSpec((B, ADA_TN), lambda j: (0, j)),
        out_shape=jax.ShapeDtypeStruct((B, N), F32),
        compiler_params=_params("arbitrary"),
        name="adaln_mod",
    )(c, w_ada, b_ada.reshape(1, N))


def _norm_kernel(x_ref, g_ref, shift_ref, scale_ref, o_ref):
    x = x_ref[...]
    ms = jnp.mean(x * x, axis=-1, keepdims=True)
    y = x * lax.rsqrt(ms + EPS) * g_ref[...]
    o_ref[...] = (y * (1.0 + scale_ref[0]) + shift_ref[0]).astype(BF16)


def _norm_mod(x2, norm_g, mod3, seq):
    T, D = x2.shape
    tiles_per_batch = seq // NORM_TM
    return pl.pallas_call(
        _norm_kernel,
        grid=(T // NORM_TM,),
        in_specs=[pl.BlockSpec((NORM_TM, D), lambda i: (i, 0)),
                  pl.BlockSpec((1, D), lambda i: (0, 0)),
                  pl.BlockSpec((1, 1, D), lambda i: (3 * (i // tiles_per_batch), 0, 0)),
                  pl.BlockSpec((1, 1, D), lambda i: (3 * (i // tiles_per_batch) + 1, 0, 0))],
        out_specs=pl.BlockSpec((NORM_TM, D), lambda i: (i, 0)),
        out_shape=jax.ShapeDtypeStruct((T, D), BF16),
        compiler_params=_params("arbitrary"),
        name="norm_mod",
    )(x2, norm_g.reshape(1, D), mod3, mod3)


def _head_rms(a, g):
    parts = []
    for hh in range(a.shape[1] // HEAD_DIM):
        c = a[:, hh * HEAD_DIM:(hh + 1) * HEAD_DIM]
        ms = jnp.mean(c * c, axis=-1, keepdims=True)
        parts.append(c * lax.rsqrt(ms + EPS))
    return jnp.concatenate(parts, axis=-1) * g


def _inproj_kernel(h_ref, w_ref, qg_ref, kg_ref, o_ref, *, group_tiles, q_scale):
    j = pl.program_id(1)
    acc = jnp.dot(h_ref[...], w_ref[...], preferred_element_type=F32)

    @pl.when(j < group_tiles)
    def _():
        o_ref[...] = (_head_rms(acc, qg_ref[...]) * q_scale).astype(BF16)

    @pl.when((j >= group_tiles) & (j < 2 * group_tiles))
    def _():
        o_ref[...] = _head_rms(acc, kg_ref[...]).astype(BF16)

    @pl.when((j >= 2 * group_tiles) & (j < 3 * group_tiles))
    def _():
        o_ref[...] = acc.astype(BF16)

    @pl.when(j >= 3 * group_tiles)
    def _():
        o_ref[...] = _silu(acc).astype(BF16)


def _in_proj(h, w_bf, q_norm_g, k_norm_g, width):
    T, D = h.shape
    gt = width // PROJ_TN
    heads_per_tile = PROJ_TN // HEAD_DIM
    qg = jnp.tile(q_norm_g.reshape(1, HEAD_DIM), (1, heads_per_tile))
    kg = jnp.tile(k_norm_g.reshape(1, HEAD_DIM), (1, heads_per_tile))
    q_scale = -(HEAD_DIM ** -0.5) * LOG2E

    def w_map(i, j):
        return (0, jnp.where(j < 4 * gt, j, j + 2 * gt))

    return pl.pallas_call(
        functools.partial(_inproj_kernel, group_tiles=gt, q_scale=q_scale),
        grid=(T // PROJ_TM, 5 * gt),
        in_specs=[pl.BlockSpec((PROJ_TM, D), lambda i, j: (i, 0)),
                  pl.BlockSpec((D, PROJ_TN), w_map),
                  pl.BlockSpec((1, PROJ_TN), lambda i, j: (0, 0)),
                  pl.BlockSpec((1, PROJ_TN), lambda i, j: (0, 0))],
        out_specs=pl.BlockSpec((PROJ_TM, PROJ_TN), lambda i, j: (i, j)),
        out_shape=jax.ShapeDtypeStruct((T, 5 * width), BF16),
        compiler_params=_params("arbitrary", "arbitrary"),
        name="in_proj",
    )(h, w_bf, qg, kg)


def _glu_kernel(h_ref, wu_ref, wg_ref, o_ref):
    h = h_ref[...]
    u = jnp.dot(h, wu_ref[...], preferred_element_type=F32)
    g = jnp.dot(h, wg_ref[...], preferred_element_type=F32)
    o_ref[...] = (u * jax.nn.sigmoid(g)).astype(BF16)


def _glu_proj(h, w_bf, width):
    T, D = h.shape
    gt = width // PROJ_TN
    return pl.pallas_call(
        _glu_kernel,
        grid=(T // PROJ_TM, gt),
        in_specs=[pl.BlockSpec((PROJ_TM, D), lambda i, j: (i, 0)),
                  pl.BlockSpec((D, PROJ_TN), lambda i, j: (0, 4 * gt + j)),
                  pl.BlockSpec((D, PROJ_TN), lambda i, j: (0, 5 * gt + j))],
        out_specs=pl.BlockSpec((PROJ_TM, PROJ_TN), lambda i, j: (i, j)),
        out_shape=jax.ShapeDtypeStruct((T, width), BF16),
        compiler_params=_params("arbitrary", "arbitrary"),
        name="glu_proj",
    )(h, w_bf, w_bf)


def _split_bf16(x):
    hi = x.astype(BF16)
    lo = (x - hi.astype(F32)).astype(BF16)
    return hi, lo


def _attn_kernel(q_ref, k_ref, v_ref, sg_ref, tri_ref, o_ref, acc_ref, r_ref):
    qi = pl.program_id(2)
    t = ATTN_T
    q = q_ref[...]
    tri = tri_ref[...]
    sign = jnp.uint32(0x80000000)

    def tile(kb, vb, causal):
        y = lax.dot_general(q, kb, (((1,), (1,)), ((), ())), preferred_element_type=F32)
        neg_abs = pltpu.bitcast(pltpu.bitcast(y, jnp.uint32) | sign, F32)
        l_stay = jnp.minimum(y, 0.0) - jnp.log(1.0 + jnp.exp2(neg_abs)) * LOG2E
        if causal is not None:
            l_stay = jnp.where(causal, l_stay, 0.0)
        hi, lo = _split_bf16(l_stay)
        cs = (jnp.dot(hi, tri, preferred_element_type=F32)
              + jnp.dot(lo, tri, preferred_element_type=F32))
        w = jnp.exp2(cs - y)
        if causal is not None:
            w = jnp.where(causal, w, 0.0)
        pv = jnp.dot(w.astype(BF16), vb, preferred_element_type=F32)
        return pv, cs[:, 0:1]

    d0 = pl.multiple_of(qi * t, t)
    rows = lax.broadcasted_iota(jnp.int32, (t, t), 0)
    cols = lax.broadcasted_iota(jnp.int32, (t, t), 1)
    pv, tot = tile(k_ref[pl.ds(d0, t), :], v_ref[pl.ds(d0, t), :], cols < rows)
    acc_ref[...] = pv
    r_ref[...] = tot

    def body(it, carry):
        s0 = pl.multiple_of((qi - 1 - it) * t, t)
        pv, tot = tile(k_ref[pl.ds(s0, t), :], v_ref[pl.ds(s0, t), :], None)
        r = r_ref[...]
        acc_ref[...] += pv * jnp.exp2(r)
        r_ref[...] = r + tot
        return carry

    lax.fori_loop(0, qi, body, 0)
    o_ref[...] = (acc_ref[...] * sg_ref[...].astype(F32)).astype(BF16)


def _sb_attention(p3, width):
    B, S, _ = p3.shape
    H = width // HEAD_DIM
    t = ATTN_T
    idx = jnp.arange(t, dtype=jnp.int32)
    tri = (idx[:, None] >= idx[None, :]).astype(BF16)
    return pl.pallas_call(
        _attn_kernel,
        grid=(B, H, S // t),
        in_specs=[pl.BlockSpec((None, t, HEAD_DIM), lambda b, h, i: (b, i, h)),
                  pl.BlockSpec((None, S, HEAD_DIM), lambda b, h, i: (b, 0, H + h)),
                  pl.BlockSpec((None, S, HEAD_DIM), lambda b, h, i: (b, 0, 2 * H + h)),
                  pl.BlockSpec((None, t, HEAD_DIM), lambda b, h, i: (b, i, 3 * H + h)),
                  pl.BlockSpec((t, t), lambda b, h, i: (0, 0))],
        out_specs=pl.BlockSpec((None, t, HEAD_DIM), lambda b, h, i: (b, i, h)),
        out_shape=jax.ShapeDtypeStruct((B, S, width), BF16),
        scratch_shapes=[pltpu.VMEM((t, HEAD_DIM), F32), pltpu.VMEM((t, 1), F32)],
        compiler_params=_params("arbitrary", "arbitrary", "arbitrary"),
        name="sb_attn",
    )(p3, p3, p3, p3, tri)


def _conv_kernel(h_ref, prev_ref, sgc_ref, wdw_ref, bdw_ref, lng_ref, lnb_ref, wpw_ref, bpw_ref,
                 o_ref, hp_ref, cv_ref, a_ref):
    si = pl.program_id(1)
    ts, C = h_ref.shape
    chunks = C // LANES
    first_tap_row = HALO_ROWS - (CONV_KERNEL - 1)

    halo = jnp.where(si == 0, 0.0, prev_ref[...].astype(F32))
    cur = h_ref[...].astype(F32)
    for c in range(chunks):
        lanes = slice(c * LANES, (c + 1) * LANES)
        hp_ref[c, 0:HALO_ROWS, :] = halo[:, lanes]
        hp_ref[c, HALO_ROWS:, :] = cur[:, lanes]

    groups = CONV_RC // SUBLANES

    def chunk_body(c, carry):
        bias = bdw_ref[c]

        def row_body(r, carry):
            r0 = pl.multiple_of(r * CONV_RC, CONV_RC)
            window = hp_ref.at[c, pl.ds(r0, CONV_RC + HALO_ROWS)]
            accs = [bias] * groups
            for k in range(CONV_KERNEL):
                wv = wdw_ref[c, k]
                for g in range(groups):
                    start = SUBLANES * g + first_tap_row + k
                    accs[g] = accs[g] + window[start:start + SUBLANES, :] * wv
            for g in range(groups):
                cv_ref[c, pl.ds(r0 + SUBLANES * g, SUBLANES), :] = accs[g]
            return carry

        return lax.fori_loop(0, ts // CONV_RC, row_body, carry)

    lax.fori_loop(0, chunks, chunk_body, 0)

    cv = cv_ref[...]
    mu = jnp.sum(jnp.sum(cv, axis=0), axis=-1, keepdims=True) * (1.0 / C)
    xc = cv - mu
    var = jnp.sum(jnp.sum(xc * xc, axis=0), axis=-1, keepdims=True) * (1.0 / C)
    rstd = lax.rsqrt(var + EPS)
    for c in range(chunks):
        lanes = slice(c * LANES, (c + 1) * LANES)
        yn = xc[c] * rstd * lng_ref[:, lanes] + lnb_ref[:, lanes]
        a_ref[:, lanes] = _silu(yn).astype(BF16)
    y = jnp.dot(a_ref[...], wpw_ref[...], preferred_element_type=F32) + bpw_ref[...]
    o_ref[...] = (y * sgc_ref[...].astype(F32)).astype(BF16)


def _conv_mix(hglu, p2, w_dw, b_dw, ln_g, ln_b, w_pw_bf, b_pw, batch, seq, width):
    T, C = hglu.shape
    ts = CONV_TS
    tiles = seq // ts
    halo_per_tile = ts // HALO_ROWS
    chunks = C // LANES
    wdw_b = jnp.broadcast_to(w_dw.reshape(CONV_KERNEL, chunks, 1, LANES).transpose(1, 0, 2, 3),
                             (chunks, CONV_KERNEL, SUBLANES, LANES))
    bdw_b = jnp.broadcast_to(b_dw.reshape(chunks, 1, LANES), (chunks, SUBLANES, LANES))
    row = lambda b, s: b * tiles + s
    vec = lambda a: a.reshape(1, C)
    const2 = lambda b, s: (0, 0)
    return pl.pallas_call(
        _conv_kernel,
        grid=(batch, tiles),
        in_specs=[pl.BlockSpec((ts, C), lambda b, s: (row(b, s), 0)),
                  pl.BlockSpec((HALO_ROWS, C), lambda b, s: (jnp.maximum(row(b, s) * halo_per_tile - 1, 0), 0)),
                  pl.BlockSpec((ts, C), lambda b, s: (row(b, s), 4)),
                  pl.BlockSpec((chunks, CONV_KERNEL, SUBLANES, LANES), lambda b, s: (0, 0, 0, 0)),
                  pl.BlockSpec((chunks, SUBLANES, LANES), lambda b, s: (0, 0, 0)),
                  pl.BlockSpec((1, C), const2), pl.BlockSpec((1, C), const2),
                  pl.BlockSpec((C, C), const2), pl.BlockSpec((1, C), const2)],
        out_specs=pl.BlockSpec((ts, C), lambda b, s: (row(b, s), 0)),
        out_shape=jax.ShapeDtypeStruct((T, C), BF16),
        scratch_shapes=[pltpu.VMEM((chunks, HALO_ROWS + ts, LANES), F32),
                        pltpu.VMEM((chunks, ts, LANES), F32),
                        pltpu.VMEM((ts, C), BF16)],
        compiler_params=_params("arbitrary", "arbitrary"),
        name="conv_mix",
    )(hglu, hglu, p2, wdw_b, bdw_b, vec(ln_g), vec(ln_b), w_pw_bf, vec(b_pw))


def _outproj_kernel(ysb_ref, ycv_ref, wt_ref, wb_ref, x_ref, gate_ref, o_ref):
    y = (jnp.dot(ysb_ref[...], wt_ref[...], preferred_element_type=F32)
         + jnp.dot(ycv_ref[...], wb_ref[...], preferred_element_type=F32))
    o_ref[...] = x_ref[...] + gate_ref[0] * y


def _out_proj(y_sb, y_conv, w_out_bf, x2, mod3, seq):
    T, W = y_sb.shape
    D = x2.shape[1]
    tiles_per_batch = seq // OUT_TM
    return pl.pallas_call(
        _outproj_kernel,
        grid=(T // OUT_TM, D // OUT_TN),
        in_specs=[pl.BlockSpec((OUT_TM, W), lambda i, j: (i, 0)),
                  pl.BlockSpec((OUT_TM, W), lambda i, j: (i, 0)),
                  pl.BlockSpec((W, OUT_TN), lambda i, j: (0, j)),
                  pl.BlockSpec((W, OUT_TN), lambda i, j: (1, j)),
                  pl.BlockSpec((OUT_TM, OUT_TN), lambda i, j: (i, j)),
                  pl.BlockSpec((1, 1, OUT_TN), lambda i, j: (3 * (i // tiles_per_batch) + 2, 0, j))],
        out_specs=pl.BlockSpec((OUT_TM, OUT_TN), lambda i, j: (i, j)),
        out_shape=jax.ShapeDtypeStruct((T, D), F32),
        compiler_params=_params("arbitrary", "arbitrary"),
        name="out_proj",
    )(y_sb, y_conv, w_out_bf, w_out_bf, x2, mod3)


def _layer(x, c, norm_g, w_ada, b_ada, w_in, q_norm_g, k_norm_g, w_dw, b_dw, ln_g, ln_b, w_pw, b_pw, w_out):
    B, S, D = x.shape
    width = w_pw.shape[0]
    assert w_in.shape[1] == 7 * width and w_out.shape[0] == 2 * width
    x2 = x.reshape(B * S, D)

    mod = _adaln_mod(c, w_ada, b_ada)
    mod3 = mod.reshape(B * 3, 1, D)
    h = _norm_mod(x2, norm_g, mod3, S)

    w_in_bf = w_in.astype(BF16)
    p2 = _in_proj(h, w_in_bf, q_norm_g, k_norm_g, width)
    hglu = _glu_proj(h, w_in_bf, width)

    y_sb = _sb_attention(p2.reshape(B, S, 5 * width), width).reshape(B * S, width)
    y_conv = _conv_mix(hglu, p2, w_dw, b_dw, ln_g, ln_b, w_pw.astype(BF16), b_pw, B, S, width)

    out = _out_proj(y_sb, y_conv, w_out.astype(BF16), x2, mod3, S)
    return out.reshape(B, S, D)


def kernel(x, c, norm_g, w_ada, b_ada, w_in, q_norm_g, k_norm_g, w_dw, b_dw, ln_g, ln_b, w_pw, b_pw, w_out):
    for layer in range(norm_g.shape[0]):
        x = _layer(x, c, norm_g[layer], w_ada[layer], b_ada[layer], w_in[layer], q_norm_g[layer],
                   k_norm_g[layer], w_dw[layer], b_dw[layer], ln_g[layer], ln_b[layer], w_pw[layer],
                   b_pw[layer], w_out[layer])
    return x
```

```python
import functools
import math

import jax
import jax.numpy as jnp
from jax import lax
from jax.experimental import pallas as pl
from jax.experimental.pallas import tpu as pltpu

F32 = jnp.float32
BF16 = jnp.bfloat16

HEAD_DIM = 128
CONV_KERNEL = 31
EPS = 1e-6
LOG2E = math.log2(math.e)

LANES = 128
SUBLANES = 8
HALO_ROWS = 32
VMEM_LIMIT = 56 * 1024 * 1024

NORM_TM = 512
PROJ_TM, PROJ_TN = 1024, 512
ATTN_TQ, ATTN_TK = 1024, 256
assert (ATTN_TQ // ATTN_TK) % 2 == 0
SOFTPLUS_LINEAR_ABOVE = 64.0
CONV_TS = 256
CONV_RC = 64
OUT_TM, OUT_TN = 1024, 512
ADA_TN = 512


def _params(*sem, flags=None):
    return pltpu.CompilerParams(dimension_semantics=sem, vmem_limit_bytes=VMEM_LIMIT, flags=flags)


def _silu(x):
    return x * jax.nn.sigmoid(x)


def _adaln_kernel(c_ref, w_ref, b_ref, o_ref):
    a = _silu(c_ref[...]).astype(BF16)
    o_ref[...] = jnp.dot(a, w_ref[...].astype(BF16), preferred_element_type=F32) + b_ref[...]


def _adaln_mod(c, w_ada, b_ada):
    B, D = c.shape
    N = w_ada.shape[1]
    return pl.pallas_call(
        _adaln_kernel,
        grid=(N // ADA_TN,),
        in_specs=[pl.BlockSpec((B, D), lambda j: (0, 0)),
                  pl.BlockSpec((D, ADA_TN), lambda j: (0, j)),
                  pl.BlockSpec((1, ADA_TN), lambda j: (0, j))],
        out_specs=pl.BlockSpec((B, ADA_TN), lambda j: (0, j)),
        out_shape=jax.ShapeDtypeStruct((B, N), F32),
        compiler_params=_params("arbitrary"),
        name="adaln_mod",
    )(c, w_ada, b_ada.reshape(1, N))


def _norm_kernel(x_ref, g_ref, shift_ref, scale_ref, o_ref):
    x = x_ref[...]
    ms = jnp.mean(x * x, axis=-1, keepdims=True)
    y = x * lax.rsqrt(ms + EPS) * g_ref[...]
    o_ref[...] = (y * (1.0 + scale_ref[0]) + shift_ref[0]).astype(BF16)


def _norm_mod(x2, norm_g, mod3, seq):
    T, D = x2.shape
    tiles_per_batch = seq // NORM_TM
    return pl.pallas_call(
        _norm_kernel,
        grid=(T // NORM_TM,),
        in_specs=[pl.BlockSpec((NORM_TM, D), lambda i: (i, 0)),
                  pl.BlockSpec((1, D), lambda i: (0, 0)),
                  pl.BlockSpec((1, 1, D), lambda i: (3 * (i // tiles_per_batch), 0, 0)),
                  pl.BlockSpec((1, 1, D), lambda i: (3 * (i // tiles_per_batch) + 1, 0, 0))],
        out_specs=pl.BlockSpec((NORM_TM, D), lambda i: (i, 0)),
        out_shape=jax.ShapeDtypeStruct((T, D), BF16),
        compiler_params=_params("arbitrary"),
        name="norm_mod",
    )(x2, norm_g.reshape(1, D), mod3, mod3)


def _head_rms(a, g):
    parts = []
    for hh in range(a.shape[1] // HEAD_DIM):
        c = a[:, hh * HEAD_DIM:(hh + 1) * HEAD_DIM]
        ms = jnp.mean(c * c, axis=-1, keepdims=True)
        parts.append(c * lax.rsqrt(ms + EPS))
    return jnp.concatenate(parts, axis=-1) * g


def _inproj_kernel(h_ref, w_ref, qg_ref, kg_ref, o_ref, *, group_tiles, q_scale):
    j = pl.program_id(1)
    acc = jnp.dot(h_ref[...], w_ref[...], preferred_element_type=F32)

    @pl.when(j < group_tiles)
    def _():
        o_ref[...] = (_head_rms(acc, qg_ref[...]) * q_scale).astype(BF16)

    @pl.when((j >= group_tiles) & (j < 2 * group_tiles))
    def _():
        o_ref[...] = _head_rms(acc, kg_ref[...]).astype(BF16)

    @pl.when((j >= 2 * group_tiles) & (j < 3 * group_tiles))
    def _():
        o_ref[...] = acc.astype(BF16)

    @pl.when(j >= 3 * group_tiles)
    def _():
        o_ref[...] = _silu(acc).astype(BF16)


def _in_proj(h, w_bf, q_norm_g, k_norm_g, width):
    T, D = h.shape
    gt = width // PROJ_TN
    heads_per_tile = PROJ_TN // HEAD_DIM
    qg = jnp.tile(q_norm_g.reshape(1, HEAD_DIM), (1, heads_per_tile))
    kg = jnp.tile(k_norm_g.reshape(1, HEAD_DIM), (1, heads_per_tile))
    q_scale = (HEAD_DIM ** -0.5) * LOG2E

    def w_map(i, j):
        return (0, jnp.where(j < 4 * gt, j, j + 2 * gt))

    return pl.pallas_call(
        functools.partial(_inproj_kernel, group_tiles=gt, q_scale=q_scale),
        grid=(T // PROJ_TM, 5 * gt),
        in_specs=[pl.BlockSpec((PROJ_TM, D), lambda i, j: (i, 0)),
                  pl.BlockSpec((D, PROJ_TN), w_map),
                  pl.BlockSpec((1, PROJ_TN), lambda i, j: (0, 0)),
                  pl.BlockSpec((1, PROJ_TN), lambda i, j: (0, 0))],
        out_specs=pl.BlockSpec((PROJ_TM, PROJ_TN), lambda i, j: (i, j)),
        out_shape=jax.ShapeDtypeStruct((T, 5 * width), BF16),
        compiler_params=_params("arbitrary", "arbitrary"),
        name="in_proj",
    )(h, w_bf, qg, kg)


def _glu_kernel(h_ref, wu_ref, wg_ref, o_ref):
    h = h_ref[...]
    u = jnp.dot(h, wu_ref[...], preferred_element_type=F32)
    g = jnp.dot(h, wg_ref[...], preferred_element_type=F32)
    o_ref[...] = (u * jax.nn.sigmoid(g)).astype(BF16)


def _glu_proj(h, w_bf, width):
    T, D = h.shape
    gt = width // PROJ_TN
    return pl.pallas_call(
        _glu_kernel,
        grid=(T // PROJ_TM, gt),
        in_specs=[pl.BlockSpec((PROJ_TM, D), lambda i, j: (i, 0)),
                  pl.BlockSpec((D, PROJ_TN), lambda i, j: (0, 4 * gt + j)),
                  pl.BlockSpec((D, PROJ_TN), lambda i, j: (0, 5 * gt + j))],
        out_specs=pl.BlockSpec((PROJ_TM, PROJ_TN), lambda i, j: (i, j)),
        out_shape=jax.ShapeDtypeStruct((T, width), BF16),
        compiler_params=_params("arbitrary", "arbitrary"),
        name="glu_proj",
    )(h, w_bf, w_bf)


def _attn_kernel(q_ref, k_ref, v_ref, sg_ref, tri_ref, o_ref, acc_ref, r_ref, z_ref, c_ref, w_ref):
    qi = pl.program_id(2)
    tq, tk = ATTN_TQ, ATTN_TK
    sub = tq // tk
    tri = tri_ref[...]

    def scores(rows, k0):
        return lax.dot_general(q_ref[rows, :], k_ref[pl.ds(k0, tk), :], (((1,), (1,)), ((), ())),
                               preferred_element_type=F32)

    def cost(z):
        return jnp.where(z > SOFTPLUS_LINEAR_ABOVE, z, jnp.log(1.0 + jnp.exp2(z)) * LOG2E)

    def causal(shape):
        return lax.broadcasted_iota(jnp.int32, shape, 1) < lax.broadcasted_iota(jnp.int32, shape, 0)

    def stage_scores(tile, slot):
        row0, k0, masked = tile
        rows = slice(row0, tq)
        z = scores(rows, k0)
        c = cost(z)
        if masked:
            c = jnp.where(causal(z.shape), c, 0.0)
        z_ref[slot, rows, :] = z
        c_ref[slot, rows, :] = c.astype(BF16)

    def stage_weights(tile, slot):
        row0, _, masked = tile
        rows = slice(row0, tq)
        cs = jnp.dot(c_ref[slot, rows, :], tri, preferred_element_type=F32)
        r = r_ref[rows, :]
        w = jnp.exp2(z_ref[slot, rows, :] - cs - r)
        if masked:
            w = jnp.where(causal(w.shape), w, 0.0)
        w_ref[slot, rows, :] = w.astype(BF16)
        r_ref[rows, :] = r + cs[:, 0:1]

    def stage_values(tile, slot):
        row0, k0, _ = tile
        rows = slice(row0, tq)
        acc_ref[rows, :] += jnp.dot(w_ref[slot, rows, :], v_ref[pl.ds(k0, tk), :], preferred_element_type=F32)

    def pipeline_steps(tiles, steps):
        for s in steps:
            if 0 <= s < len(tiles):
                stage_scores(tiles[s], s % 2)
            if 0 <= s - 1 < len(tiles):
                stage_weights(tiles[s - 1], (s - 1) % 2)
            if 0 <= s - 2 < len(tiles):
                stage_values(tiles[s - 2], (s - 2) % 2)

    acc_ref[...] = jnp.zeros_like(acc_ref)
    r_ref[...] = jnp.zeros_like(r_ref)
    diag = [(d * tk, pl.multiple_of(qi * tq + d * tk, tk), True) for d in reversed(range(sub))]

    @pl.when(qi == 0)
    def _():
        pipeline_steps(diag, range(sub + 2))

    @pl.when(qi > 0)
    def _():
        n = qi * sub
        early = lambda j: (0, pl.multiple_of((n - 1 - j) * tk, tk), False)
        pipeline_steps(diag + [early(0), early(1)], range(sub + 2))

        def body(p, carry):
            j = 2 * p
            stage_scores(early(j + 2), 0)
            stage_weights(early(j + 1), 1)
            stage_values(early(j), 0)
            stage_scores(early(j + 3), 1)
            stage_weights(early(j + 2), 0)
            stage_values(early(j + 1), 1)
            return carry

        lax.fori_loop(0, n // 2 - 1, body, 0)
        stage_weights(early(n - 1), 1)
        stage_values(early(n - 2), 0)
        stage_values(early(n - 1), 1)

    o_ref[...] = (acc_ref[...] * sg_ref[...].astype(F32)).astype(BF16)


def _sb_attention(p3, width):
    B, S, _ = p3.shape
    H = width // HEAD_DIM
    tq, tk = ATTN_TQ, ATTN_TK
    idx = jnp.arange(tk, dtype=jnp.int32)
    tri = (idx[:, None] >= idx[None, :]).astype(BF16)
    return pl.pallas_call(
        _attn_kernel,
        grid=(B, H, S // tq),
        in_specs=[pl.BlockSpec((None, tq, HEAD_DIM), lambda b, h, i: (b, i, h)),
                  pl.BlockSpec((None, S, HEAD_DIM), lambda b, h, i: (b, 0, H + h)),
                  pl.BlockSpec((None, S, HEAD_DIM), lambda b, h, i: (b, 0, 2 * H + h)),
                  pl.BlockSpec((None, tq, HEAD_DIM), lambda b, h, i: (b, i, 3 * H + h)),
                  pl.BlockSpec(tri.shape, lambda b, h, i: (0, 0))],
        out_specs=pl.BlockSpec((None, tq, HEAD_DIM), lambda b, h, i: (b, i, h)),
        out_shape=jax.ShapeDtypeStruct((B, S, width), BF16),
        scratch_shapes=[pltpu.VMEM((tq, HEAD_DIM), F32), pltpu.VMEM((tq, 1), F32),
                        pltpu.VMEM((2, tq, tk), F32), pltpu.VMEM((2, tq, tk), BF16),
                        pltpu.VMEM((2, tq, tk), BF16)],
        compiler_params=_params("arbitrary", "arbitrary", "arbitrary"),
        name="sb_attn",
    )(p3, p3, p3, p3, tri)


def _conv_kernel(h_ref, prev_ref, sgc_ref, wdw_ref, bdw_ref, lng_ref, lnb_ref, wpw_ref, bpw_ref,
                 o_ref, hp_ref, cv_ref, a_ref):
    si = pl.program_id(1)
    ts, C = h_ref.shape
    chunks = C // LANES
    first_tap_row = HALO_ROWS - (CONV_KERNEL - 1)

    halo = jnp.where(si == 0, 0.0, prev_ref[...].astype(F32))
    cur = h_ref[...].astype(F32)
    for c in range(chunks):
        lanes = slice(c * LANES, (c + 1) * LANES)
        hp_ref[c, 0:HALO_ROWS, :] = halo[:, lanes]
        hp_ref[c, HALO_ROWS:, :] = cur[:, lanes]

    groups = CONV_RC // SUBLANES

    def chunk_body(c, carry):
        bias = bdw_ref[c]

        def row_body(r, carry):
            r0 = pl.multiple_of(r * CONV_RC, CONV_RC)
            window = hp_ref.at[c, pl.ds(r0, CONV_RC + HALO_ROWS)]
            accs = [bias] * groups
            for k in range(CONV_KERNEL):
                wv = wdw_ref[c, k]
                for g in range(groups):
                    start = SUBLANES * g + first_tap_row + k
                    accs[g] = accs[g] + window[start:start + SUBLANES, :] * wv
            for g in range(groups):
                cv_ref[c, pl.ds(r0 + SUBLANES * g, SUBLANES), :] = accs[g]
            return carry

        return lax.fori_loop(0, ts // CONV_RC, row_body, carry)

    lax.fori_loop(0, chunks, chunk_body, 0)

    cv = cv_ref[...]
    mu = jnp.sum(jnp.sum(cv, axis=0), axis=-1, keepdims=True) * (1.0 / C)
    xc = cv - mu
    var = jnp.sum(jnp.sum(xc * xc, axis=0), axis=-1, keepdims=True) * (1.0 / C)
    rstd = lax.rsqrt(var + EPS)
    for c in range(chunks):
        lanes = slice(c * LANES, (c + 1) * LANES)
        yn = xc[c] * rstd * lng_ref[:, lanes] + lnb_ref[:, lanes]
        a_ref[:, lanes] = _silu(yn).astype(BF16)
    y = jnp.dot(a_ref[...], wpw_ref[...], preferred_element_type=F32) + bpw_ref[...]
    o_ref[...] = (y * sgc_ref[...].astype(F32)).astype(BF16)


def _conv_mix(hglu, p2, w_dw, b_dw, ln_g, ln_b, w_pw_bf, b_pw, batch, seq, width):
    T, C = hglu.shape
    ts = CONV_TS
    tiles = seq // ts
    halo_per_tile = ts // HALO_ROWS
    chunks = C // LANES
    wdw_b = jnp.broadcast_to(w_dw.reshape(CONV_KERNEL, chunks, 1, LANES).transpose(1, 0, 2, 3),
                             (chunks, CONV_KERNEL, SUBLANES, LANES))
    bdw_b = jnp.broadcast_to(b_dw.reshape(chunks, 1, LANES), (chunks, SUBLANES, LANES))
    row = lambda b, s: b * tiles + s
    vec = lambda a: a.reshape(1, C)
    const2 = lambda b, s: (0, 0)
    return pl.pallas_call(
        _conv_kernel,
        grid=(batch, tiles),
        in_specs=[pl.BlockSpec((ts, C), lambda b, s: (row(b, s), 0)),
                  pl.BlockSpec((HALO_ROWS, C), lambda b, s: (jnp.maximum(row(b, s) * halo_per_tile - 1, 0), 0)),
                  pl.BlockSpec((ts, C), lambda b, s: (row(b, s), 4)),
                  pl.BlockSpec((chunks, CONV_KERNEL, SUBLANES, LANES), lambda b, s: (0, 0, 0, 0)),
                  pl.BlockSpec((chunks, SUBLANES, LANES), lambda b, s: (0, 0, 0)),
                  pl.BlockSpec((1, C), const2), pl.BlockSpec((1, C), const2),
                  pl.BlockSpec((C, C), const2), pl.BlockSpec((1, C), const2)],
        out_specs=pl.BlockSpec((ts, C), lambda b, s: (row(b, s), 0)),
        out_shape=jax.ShapeDtypeStruct((T, C), BF16),
        scratch_shapes=[pltpu.VMEM((chunks, HALO_ROWS + ts, LANES), F32),
                        pltpu.VMEM((chunks, ts, LANES), F32),
                        pltpu.VMEM((ts, C), BF16)],
        compiler_params=_params("arbitrary", "arbitrary"),
        name="conv_mix",
    )(hglu, hglu, p2, wdw_b, bdw_b, vec(ln_g), vec(ln_b), w_pw_bf, vec(b_pw))


def _outproj_kernel(ysb_ref, ycv_ref, wt_ref, wb_ref, x_ref, gate_ref, o_ref):
    y = (jnp.dot(ysb_ref[...], wt_ref[...], preferred_element_type=F32)
         + jnp.dot(ycv_ref[...], wb_ref[...], preferred_element_type=F32))
    o_ref[...] = x_ref[...] + gate_ref[0] * y


def _out_proj(y_sb, y_conv, w_out_bf, x2, mod3, seq):
    T, W = y_sb.shape
    D = x2.shape[1]
    tiles_per_batch = seq // OUT_TM
    return pl.pallas_call(
        _outproj_kernel,
        grid=(T // OUT_TM, D // OUT_TN),
        in_specs=[pl.BlockSpec((OUT_TM, W), lambda i, j: (i, 0)),
                  pl.BlockSpec((OUT_TM, W), lambda i, j: (i, 0)),
                  pl.BlockSpec((W, OUT_TN), lambda i, j: (0, j)),
                  pl.BlockSpec((W, OUT_TN), lambda i, j: (1, j)),
                  pl.BlockSpec((OUT_TM, OUT_TN), lambda i, j: (i, j)),
                  pl.BlockSpec((1, 1, OUT_TN), lambda i, j: (3 * (i // tiles_per_batch) + 2, 0, j))],
        out_specs=pl.BlockSpec((OUT_TM, OUT_TN), lambda i, j: (i, j)),
        out_shape=jax.ShapeDtypeStruct((T, D), F32),
        compiler_params=_params("arbitrary", "arbitrary"),
        name="out_proj",
    )(y_sb, y_conv, w_out_bf, w_out_bf, x2, mod3)


def _layer(x, c, norm_g, w_ada, b_ada, w_in, q_norm_g, k_norm_g, w_dw, b_dw, ln_g, ln_b, w_pw, b_pw, w_out):
    B, S, D = x.shape
    width = w_pw.shape[0]
    assert w_in.shape[1] == 7 * width and w_out.shape[0] == 2 * width
    x2 = x.reshape(B * S, D)

    mod = _adaln_mod(c, w_ada, b_ada)
    mod3 = mod.reshape(B * 3, 1, D)
    h = _norm_mod(x2, norm_g, mod3, S)

    w_in_bf = w_in.astype(BF16)
    p2 = _in_proj(h, w_in_bf, q_norm_g, k_norm_g, width)
    hglu = _glu_proj(h, w_in_bf, width)

    y_sb = _sb_attention(p2.reshape(B, S, 5 * width), width).reshape(B * S, width)
    y_conv = _conv_mix(hglu, p2, w_dw, b_dw, ln_g, ln_b, w_pw.astype(BF16), b_pw, B, S, width)

    out = _out_proj(y_sb, y_conv, w_out.astype(BF16), x2, mod3, S)
    return out.reshape(B, S, D)


def kernel(x, c, norm_g, w_ada, b_ada, w_in, q_norm_g, k_norm_g, w_dw, b_dw, ln_g, ln_b, w_pw, b_pw, w_out):
    for layer in range(norm_g.shape[0]):
        x = _layer(x, c, norm_g[layer], w_ada[layer], b_ada[layer], w_in[layer], q_norm_g[layer],
                   k_norm_g[layer], w_dw[layer], b_dw[layer], ln_g[layer], ln_b[layer], w_pw[layer],
                   b_pw[layer], w_out[layer])
    return x
```

```python
import functools
import math

import jax
import jax.numpy as jnp
from jax import lax
from jax.experimental import pallas as pl
from jax.experimental.pallas import tpu as pltpu

F32 = jnp.float32
BF16 = jnp.bfloat16

HEAD_DIM = 128
CONV_KERNEL = 31
EPS = 1e-6
LOG2E = math.log2(math.e)

LANES = 128
SUBLANES = 8
HALO_ROWS = 32
VMEM_LIMIT = 56 * 1024 * 1024

NORM_TM = 512
PROJ_TM, PROJ_TN = 1024, 512
ATTN_TQ, ATTN_TK = 1024, 256
assert (ATTN_TQ // ATTN_TK) % 2 == 0
SOFTPLUS_LINEAR_ABOVE = 64.0
CONV_TS = 512
CONV_CHAINS = 2
OUT_TM, OUT_TN = 1024, 512
ADA_TN = 512


def _params(*sem, flags=None):
    return pltpu.CompilerParams(dimension_semantics=sem, vmem_limit_bytes=VMEM_LIMIT, flags=flags)


def _silu(x):
    return x * jax.nn.sigmoid(x)


def _adaln_kernel(c_ref, w_ref, b_ref, o_ref):
    a = _silu(c_ref[...]).astype(BF16)
    o_ref[...] = jnp.dot(a, w_ref[...].astype(BF16), preferred_element_type=F32) + b_ref[...]


def _adaln_mod(c, w_ada, b_ada):
    B, D = c.shape
    N = w_ada.shape[1]
    return pl.pallas_call(
        _adaln_kernel,
        grid=(N // ADA_TN,),
        in_specs=[pl.BlockSpec((B, D), lambda j: (0, 0)),
                  pl.BlockSpec((D, ADA_TN), lambda j: (0, j)),
                  pl.BlockSpec((1, ADA_TN), lambda j: (0, j))],
        out_specs=pl.BlockSpec((B, ADA_TN), lambda j: (0, j)),
        out_shape=jax.ShapeDtypeStruct((B, N), F32),
        compiler_params=_params("arbitrary"),
        name="adaln_mod",
    )(c, w_ada, b_ada.reshape(1, N))


def _norm_kernel(x_ref, g_ref, shift_ref, scale_ref, o_ref):
    x = x_ref[...]
    ms = jnp.mean(x * x, axis=-1, keepdims=True)
    y = x * lax.rsqrt(ms + EPS) * g_ref[...]
    o_ref[...] = (y * (1.0 + scale_ref[0]) + shift_ref[0]).astype(BF16)


def _norm_mod(x2, norm_g, mod3, seq):
    T, D = x2.shape
    tiles_per_batch = seq // NORM_TM
    return pl.pallas_call(
        _norm_kernel,
        grid=(T // NORM_TM,),
        in_specs=[pl.BlockSpec((NORM_TM, D), lambda i: (i, 0)),
                  pl.BlockSpec((1, D), lambda i: (0, 0)),
                  pl.BlockSpec((1, 1, D), lambda i: (3 * (i // tiles_per_batch), 0, 0)),
                  pl.BlockSpec((1, 1, D), lambda i: (3 * (i // tiles_per_batch) + 1, 0, 0))],
        out_specs=pl.BlockSpec((NORM_TM, D), lambda i: (i, 0)),
        out_shape=jax.ShapeDtypeStruct((T, D), BF16),
        compiler_params=_params("arbitrary"),
        name="norm_mod",
    )(x2, norm_g.reshape(1, D), mod3, mod3)


def _head_rms(a, g):
    parts = []
    for hh in range(a.shape[1] // HEAD_DIM):
        c = a[:, hh * HEAD_DIM:(hh + 1) * HEAD_DIM]
        ms = jnp.mean(c * c, axis=-1, keepdims=True)
        parts.append(c * lax.rsqrt(ms + EPS))
    return jnp.concatenate(parts, axis=-1) * g


def _depthwise_chunk(hg_ref, halo_ref, wdw_ref, bdw_ref, cv_ref, hp_ref, sequence_start, never):
    rows = hg_ref.shape[0]
    first_tap_row = HALO_ROWS - (CONV_KERNEL - 1)
    hp_ref[0:HALO_ROWS, :] = jnp.where(sequence_start, 0.0, halo_ref[...].astype(F32))
    hp_ref[HALO_ROWS:, :] = hg_ref[...].astype(F32)
    bias = bdw_ref[0]
    taps = [wdw_ref[0, k] for k in range(CONV_KERNEL)]
    prev = [bias] * CONV_CHAINS
    for g in range(rows // SUBLANES):
        acc = jnp.where(never, prev[g % CONV_CHAINS], bias)
        for k in range(CONV_KERNEL):
            start = SUBLANES * g + first_tap_row + k
            acc = acc + hp_ref[start:start + SUBLANES, :] * taps[k]
        cv_ref[SUBLANES * g:SUBLANES * (g + 1), :] = acc
        prev[g % CONV_CHAINS] = acc


def _inproj_kernel(h_ref, w_ref, qg_ref, kg_ref, hg_ref, halo_ref, wdw_ref, bdw_ref, o_ref, cv_ref, hp_ref,
                   *, group_tiles, q_scale, tiles_per_seq):
    i = pl.program_id(0)
    j = pl.program_id(1)
    acc = jnp.dot(h_ref[...], w_ref[...], preferred_element_type=F32)
    _depthwise_chunk(hg_ref, halo_ref, wdw_ref, bdw_ref, cv_ref, hp_ref, i % tiles_per_seq == 0, i < 0)

    @pl.when(j < group_tiles)
    def _():
        o_ref[...] = (_head_rms(acc, qg_ref[...]) * q_scale).astype(BF16)

    @pl.when((j >= group_tiles) & (j < 2 * group_tiles))
    def _():
        o_ref[...] = _head_rms(acc, kg_ref[...]).astype(BF16)

    @pl.when((j >= 2 * group_tiles) & (j < 3 * group_tiles))
    def _():
        o_ref[...] = acc.astype(BF16)

    @pl.when(j >= 3 * group_tiles)
    def _():
        o_ref[...] = _silu(acc).astype(BF16)


def _in_proj(h, w_bf, q_norm_g, k_norm_g, hglu, w_dw, b_dw, width, seq):
    T, D = h.shape
    gt = width // PROJ_TN
    heads_per_tile = PROJ_TN // HEAD_DIM
    qg = jnp.tile(q_norm_g.reshape(1, HEAD_DIM), (1, heads_per_tile))
    kg = jnp.tile(k_norm_g.reshape(1, HEAD_DIM), (1, heads_per_tile))
    q_scale = (HEAD_DIM ** -0.5) * LOG2E
    chunks = width // LANES
    assert chunks <= 5 * gt and seq % PROJ_TM == 0
    halo_per_tile = PROJ_TM // HALO_ROWS
    wdw_b = jnp.broadcast_to(w_dw.reshape(CONV_KERNEL, chunks, 1, LANES).transpose(1, 0, 2, 3),
                             (chunks, CONV_KERNEL, SUBLANES, LANES))
    bdw_b = jnp.broadcast_to(b_dw.reshape(chunks, 1, LANES), (chunks, SUBLANES, LANES))

    def w_map(i, j):
        return (0, jnp.where(j < 4 * gt, j, j + 2 * gt))

    chunk = lambda j: jnp.minimum(j, chunks - 1)

    return pl.pallas_call(
        functools.partial(_inproj_kernel, group_tiles=gt, q_scale=q_scale, tiles_per_seq=seq // PROJ_TM),
        grid=(T // PROJ_TM, 5 * gt),
        in_specs=[pl.BlockSpec((PROJ_TM, D), lambda i, j: (i, 0)),
                  pl.BlockSpec((D, PROJ_TN), w_map),
                  pl.BlockSpec((1, PROJ_TN), lambda i, j: (0, 0)),
                  pl.BlockSpec((1, PROJ_TN), lambda i, j: (0, 0)),
                  pl.BlockSpec((PROJ_TM, LANES), lambda i, j: (i, chunk(j))),
                  pl.BlockSpec((HALO_ROWS, LANES), lambda i, j: (jnp.maximum(i * halo_per_tile - 1, 0), chunk(j))),
                  pl.BlockSpec((1, CONV_KERNEL, SUBLANES, LANES), lambda i, j: (chunk(j), 0, 0, 0)),
                  pl.BlockSpec((1, SUBLANES, LANES), lambda i, j: (chunk(j), 0, 0))],
        out_specs=[pl.BlockSpec((PROJ_TM, PROJ_TN), lambda i, j: (i, j)),
                   pl.BlockSpec((PROJ_TM, LANES), lambda i, j: (i, chunk(j)))],
        out_shape=[jax.ShapeDtypeStruct((T, 5 * width), BF16), jax.ShapeDtypeStruct((T, width), F32)],
        scratch_shapes=[pltpu.VMEM((HALO_ROWS + PROJ_TM, LANES), F32)],
        compiler_params=_params("arbitrary", "arbitrary"),
        name="in_proj",
    )(h, w_bf, qg, kg, hglu, hglu, wdw_b, bdw_b)


def _glu_kernel(h_ref, wu_ref, wg_ref, o_ref):
    h = h_ref[...]
    u = jnp.dot(h, wu_ref[...], preferred_element_type=F32)
    g = jnp.dot(h, wg_ref[...], preferred_element_type=F32)
    o_ref[...] = (u * jax.nn.sigmoid(g)).astype(BF16)


def _glu_proj(h, w_bf, width):
    T, D = h.shape
    gt = width // PROJ_TN
    return pl.pallas_call(
        _glu_kernel,
        grid=(T // PROJ_TM, gt),
        in_specs=[pl.BlockSpec((PROJ_TM, D), lambda i, j: (i, 0)),
                  pl.BlockSpec((D, PROJ_TN), lambda i, j: (0, 4 * gt + j)),
                  pl.BlockSpec((D, PROJ_TN), lambda i, j: (0, 5 * gt + j))],
        out_specs=pl.BlockSpec((PROJ_TM, PROJ_TN), lambda i, j: (i, j)),
        out_shape=jax.ShapeDtypeStruct((T, width), BF16),
        compiler_params=_params("arbitrary", "arbitrary"),
        name="glu_proj",
    )(h, w_bf, w_bf)


def _attn_kernel(q_ref, k_ref, v_ref, sg_ref, tri_ref, o_ref, acc_ref, r_ref, z_ref, c_ref, w_ref):
    qi = pl.program_id(2)
    tq, tk = ATTN_TQ, ATTN_TK
    sub = tq // tk
    tri = tri_ref[...]

    def scores(rows, k0):
        return lax.dot_general(q_ref[rows, :], k_ref[pl.ds(k0, tk), :], (((1,), (1,)), ((), ())),
                               preferred_element_type=F32)

    def cost(z):
        return jnp.where(z > SOFTPLUS_LINEAR_ABOVE, z, jnp.log(1.0 + jnp.exp2(z)) * LOG2E)

    def causal(shape):
        return lax.broadcasted_iota(jnp.int32, shape, 1) < lax.broadcasted_iota(jnp.int32, shape, 0)

    def stage_scores(tile, slot):
        row0, k0, masked = tile
        rows = slice(row0, tq)
        z = scores(rows, k0)
        c = cost(z)
        if masked:
            c = jnp.where(causal(z.shape), c, 0.0)
        z_ref[slot, rows, :] = z
        c_ref[slot, rows, :] = c.astype(BF16)

    def stage_weights(tile, slot):
        row0, _, masked = tile
        rows = slice(row0, tq)
        cs = jnp.dot(c_ref[slot, rows, :], tri, preferred_element_type=F32)
        r = r_ref[rows, :]
        w = jnp.exp2(z_ref[slot, rows, :] - cs - r)
        if masked:
            w = jnp.where(causal(w.shape), w, 0.0)
        w_ref[slot, rows, :] = w.astype(BF16)
        r_ref[rows, :] = r + cs[:, 0:1]

    def stage_values(tile, slot):
        row0, k0, _ = tile
        rows = slice(row0, tq)
        acc_ref[rows, :] += jnp.dot(w_ref[slot, rows, :], v_ref[pl.ds(k0, tk), :], preferred_element_type=F32)

    def pipeline_steps(tiles, steps):
        for s in steps:
            if 0 <= s < len(tiles):
                stage_scores(tiles[s], s % 2)
            if 0 <= s - 1 < len(tiles):
                stage_weights(tiles[s - 1], (s - 1) % 2)
            if 0 <= s - 2 < len(tiles):
                stage_values(tiles[s - 2], (s - 2) % 2)

    acc_ref[...] = jnp.zeros_like(acc_ref)
    r_ref[...] = jnp.zeros_like(r_ref)
    diag = [(d * tk, pl.multiple_of(qi * tq + d * tk, tk), True) for d in reversed(range(sub))]

    @pl.when(qi == 0)
    def _():
        pipeline_steps(diag, range(sub + 2))

    @pl.when(qi > 0)
    def _():
        n = qi * sub
        early = lambda j: (0, pl.multiple_of((n - 1 - j) * tk, tk), False)
        pipeline_steps(diag + [early(0), early(1)], range(sub + 2))

        def body(p, carry):
            j = 2 * p
            stage_scores(early(j + 2), 0)
            stage_weights(early(j + 1), 1)
            stage_values(early(j), 0)
            stage_scores(early(j + 3), 1)
            stage_weights(early(j + 2), 0)
            stage_values(early(j + 1), 1)
            return carry

        lax.fori_loop(0, n // 2 - 1, body, 0)
        stage_weights(early(n - 1), 1)
        stage_values(early(n - 2), 0)
        stage_values(early(n - 1), 1)

    o_ref[...] = (acc_ref[...] * sg_ref[...].astype(F32)).astype(BF16)


def _sb_attention(p3, width):
    B, S, _ = p3.shape
    H = width // HEAD_DIM
    tq, tk = ATTN_TQ, ATTN_TK
    idx = jnp.arange(tk, dtype=jnp.int32)
    tri = (idx[:, None] >= idx[None, :]).astype(BF16)
    return pl.pallas_call(
        _attn_kernel,
        grid=(B, H, S // tq),
        in_specs=[pl.BlockSpec((None, tq, HEAD_DIM), lambda b, h, i: (b, i, h)),
                  pl.BlockSpec((None, S, HEAD_DIM), lambda b, h, i: (b, 0, H + h)),
                  pl.BlockSpec((None, S, HEAD_DIM), lambda b, h, i: (b, 0, 2 * H + h)),
                  pl.BlockSpec((None, tq, HEAD_DIM), lambda b, h, i: (b, i, 3 * H + h)),
                  pl.BlockSpec(tri.shape, lambda b, h, i: (0, 0))],
        out_specs=pl.BlockSpec((None, tq, HEAD_DIM), lambda b, h, i: (b, i, h)),
        out_shape=jax.ShapeDtypeStruct((B, S, width), BF16),
        scratch_shapes=[pltpu.VMEM((tq, HEAD_DIM), F32), pltpu.VMEM((tq, 1), F32),
                        pltpu.VMEM((2, tq, tk), F32), pltpu.VMEM((2, tq, tk), BF16),
                        pltpu.VMEM((2, tq, tk), BF16)],
        compiler_params=_params("arbitrary", "arbitrary", "arbitrary"),
        name="sb_attn",
    )(p3, p3, p3, p3, tri)


def _conv_kernel(cv_ref, sgc_ref, lng_ref, lnb_ref, wpw_ref, bpw_ref, o_ref):
    cv = cv_ref[...]
    mu = jnp.mean(cv, axis=-1, keepdims=True)
    xc = cv - mu
    var = jnp.mean(xc * xc, axis=-1, keepdims=True)
    yn = xc * lax.rsqrt(var + EPS) * lng_ref[...] + lnb_ref[...]
    y = jnp.dot(_silu(yn).astype(BF16), wpw_ref[...], preferred_element_type=F32) + bpw_ref[...]
    o_ref[...] = (y * sgc_ref[...].astype(F32)).astype(BF16)


def _conv_mix(cv, p2, ln_g, ln_b, w_pw_bf, b_pw):
    T, C = cv.shape
    ts = CONV_TS
    vec = lambda a: a.reshape(1, C)
    const2 = lambda s: (0, 0)
    return pl.pallas_call(
        _conv_kernel,
        grid=(T // ts,),
        in_specs=[pl.BlockSpec((ts, C), lambda s: (s, 0)),
                  pl.BlockSpec((ts, C), lambda s: (s, 4)),
                  pl.BlockSpec((1, C), const2), pl.BlockSpec((1, C), const2),
                  pl.BlockSpec((C, C), const2), pl.BlockSpec((1, C), const2)],
        out_specs=pl.BlockSpec((ts, C), lambda s: (s, 0)),
        out_shape=jax.ShapeDtypeStruct((T, C), BF16),
        compiler_params=_params("arbitrary"),
        name="conv_mix",
    )(cv, p2, vec(ln_g), vec(ln_b), w_pw_bf, vec(b_pw))


def _outproj_kernel(ysb_ref, ycv_ref, wt_ref, wb_ref, x_ref, gate_ref, o_ref):
    y = (jnp.dot(ysb_ref[...], wt_ref[...], preferred_element_type=F32)
         + jnp.dot(ycv_ref[...], wb_ref[...], preferred_element_type=F32))
    o_ref[...] = x_ref[...] + gate_ref[0] * y


def _out_proj(y_sb, y_conv, w_out_bf, x2, mod3, seq):
    T, W = y_sb.shape
    D = x2.shape[1]
    tiles_per_batch = seq // OUT_TM
    return pl.pallas_call(
        _outproj_kernel,
        grid=(T // OUT_TM, D // OUT_TN),
        in_specs=[pl.BlockSpec((OUT_TM, W), lambda i, j: (i, 0)),
                  pl.BlockSpec((OUT_TM, W), lambda i, j: (i, 0)),
                  pl.BlockSpec((W, OUT_TN), lambda i, j: (0, j)),
                  pl.BlockSpec((W, OUT_TN), lambda i, j: (1, j)),
                  pl.BlockSpec((OUT_TM, OUT_TN), lambda i, j: (i, j)),
                  pl.BlockSpec((1, 1, OUT_TN), lambda i, j: (3 * (i // tiles_per_batch) + 2, 0, j))],
        out_specs=pl.BlockSpec((OUT_TM, OUT_TN), lambda i, j: (i, j)),
        out_shape=jax.ShapeDtypeStruct((T, D), F32),
        compiler_params=_params("arbitrary", "arbitrary"),
        name="out_proj",
    )(y_sb, y_conv, w_out_bf, w_out_bf, x2, mod3)


def _layer(x, c, norm_g, w_ada, b_ada, w_in, q_norm_g, k_norm_g, w_dw, b_dw, ln_g, ln_b, w_pw, b_pw, w_out):
    B, S, D = x.shape
    width = w_pw.shape[0]
    assert w_in.shape[1] == 7 * width and w_out.shape[0] == 2 * width
    x2 = x.reshape(B * S, D)

    mod = _adaln_mod(c, w_ada, b_ada)
    mod3 = mod.reshape(B * 3, 1, D)
    h = _norm_mod(x2, norm_g, mod3, S)

    w_in_bf = w_in.astype(BF16)
    hglu = _glu_proj(h, w_in_bf, width)
    p2, cv = _in_proj(h, w_in_bf, q_norm_g, k_norm_g, hglu, w_dw, b_dw, width, S)

    y_sb = _sb_attention(p2.reshape(B, S, 5 * width), width).reshape(B * S, width)
    y_conv = _conv_mix(cv, p2, ln_g, ln_b, w_pw.astype(BF16), b_pw)

    out = _out_proj(y_sb, y_conv, w_out.astype(BF16), x2, mod3, S)
    return out.reshape(B, S, D)


def kernel(x, c, norm_g, w_ada, b_ada, w_in, q_norm_g, k_norm_g, w_dw, b_dw, ln_g, ln_b, w_pw, b_pw, w_out):
    for layer in range(norm_g.shape[0]):
        x = _layer(x, c, norm_g[layer], w_ada[layer], b_ada[layer], w_in[layer], q_norm_g[layer],
                   k_norm_g[layer], w_dw[layer], b_dw[layer], ln_g[layer], ln_b[layer], w_pw[layer],
                   b_pw[layer], w_out[layer])
    return x
```

```python
import functools
import math

import jax
import jax.numpy as jnp
from jax import lax
from jax.experimental import pallas as pl
from jax.experimental.pallas import tpu as pltpu

F32 = jnp.float32
BF16 = jnp.bfloat16

HEAD_DIM = 128
CONV_KERNEL = 31
EPS = 1e-6
LOG2E = math.log2(math.e)

LANES = 128
SUBLANES = 8
HALO_ROWS = 32
VMEM_LIMIT = 56 * 1024 * 1024

NORM_TM = 512
PROJ_TM, PROJ_TN = 1024, 512
GLU_TN = 256
CAST_ROWS = 1024
ATTN_TQ, ATTN_TK = 1024, 256
assert (ATTN_TQ // ATTN_TK) % 2 == 0
SOFTPLUS_LINEAR_ABOVE = 64.0
CONV_TS = 512
CONV_CHAINS = 2
OUT_TM, OUT_TN = 1024, 512
ADA_TN = 512


def _params(*sem, flags=None):
    return pltpu.CompilerParams(dimension_semantics=sem, vmem_limit_bytes=VMEM_LIMIT, flags=flags)


def _sigmoid(x):
    return 0.5 * jnp.tanh(0.5 * x) + 0.5


def _silu(x):
    return x * _sigmoid(x)


def _adaln_kernel(c_ref, w_ref, b_ref, o_ref):
    a = _silu(c_ref[...]).astype(BF16)
    o_ref[...] = jnp.dot(a, w_ref[...].astype(BF16), preferred_element_type=F32) + b_ref[...]


def _adaln_mod(c, w_ada, b_ada):
    B, D = c.shape
    N = w_ada.shape[1]
    return pl.pallas_call(
        _adaln_kernel,
        grid=(N // ADA_TN,),
        in_specs=[pl.BlockSpec((B, D), lambda j: (0, 0)),
                  pl.BlockSpec((D, ADA_TN), lambda j: (0, j)),
                  pl.BlockSpec((1, ADA_TN), lambda j: (0, j))],
        out_specs=pl.BlockSpec((B, ADA_TN), lambda j: (0, j)),
        out_shape=jax.ShapeDtypeStruct((B, N), F32),
        compiler_params=_params("arbitrary"),
        name="adaln_mod",
    )(c, w_ada, b_ada.reshape(1, N))


def _norm_kernel(x_ref, g_ref, shift_ref, scale_ref, o_ref):
    x = x_ref[...]
    ms = jnp.mean(x * x, axis=-1, keepdims=True)
    y = x * lax.rsqrt(ms + EPS) * g_ref[...]
    o_ref[...] = (y * (1.0 + scale_ref[0]) + shift_ref[0]).astype(BF16)


def _norm_mod(x2, norm_g, mod3, seq):
    T, D = x2.shape
    tiles_per_batch = seq // NORM_TM
    return pl.pallas_call(
        _norm_kernel,
        grid=(T // NORM_TM,),
        in_specs=[pl.BlockSpec((NORM_TM, D), lambda i: (i, 0)),
                  pl.BlockSpec((1, D), lambda i: (0, 0)),
                  pl.BlockSpec((1, 1, D), lambda i: (3 * (i // tiles_per_batch), 0, 0)),
                  pl.BlockSpec((1, 1, D), lambda i: (3 * (i // tiles_per_batch) + 1, 0, 0))],
        out_specs=pl.BlockSpec((NORM_TM, D), lambda i: (i, 0)),
        out_shape=jax.ShapeDtypeStruct((T, D), BF16),
        compiler_params=_params("arbitrary"),
        name="norm_mod",
    )(x2, norm_g.reshape(1, D), mod3, mod3)


def _head_rms(a, g):
    parts = []
    for hh in range(a.shape[1] // HEAD_DIM):
        c = a[:, hh * HEAD_DIM:(hh + 1) * HEAD_DIM]
        ms = jnp.mean(c * c, axis=-1, keepdims=True)
        parts.append(c * lax.rsqrt(ms + EPS))
    return jnp.concatenate(parts, axis=-1) * g


def _depthwise_chunk(hg_ref, halo_ref, wdw_ref, bdw_ref, cv_ref, hp_ref, sequence_start, never):
    rows = hg_ref.shape[0]
    first_tap_row = HALO_ROWS - (CONV_KERNEL - 1)
    hp_ref[0:HALO_ROWS, :] = jnp.where(sequence_start, 0.0, halo_ref[...].astype(F32))
    hp_ref[HALO_ROWS:, :] = hg_ref[...].astype(F32)
    bias = bdw_ref[0]
    taps = [wdw_ref[0, k] for k in range(CONV_KERNEL)]
    prev = [bias] * CONV_CHAINS
    for g in range(rows // SUBLANES):
        acc = jnp.where(never, prev[g % CONV_CHAINS], bias)
        for k in range(CONV_KERNEL):
            start = SUBLANES * g + first_tap_row + k
            acc = acc + hp_ref[start:start + SUBLANES, :] * taps[k]
        cv_ref[SUBLANES * g:SUBLANES * (g + 1), :] = acc
        prev[g % CONV_CHAINS] = acc


def _inproj_kernel(h_ref, w_ref, qg_ref, kg_ref, hg_ref, halo_ref, wdw_ref, bdw_ref, o_ref, cv_ref, hp_ref,
                   *, group_tiles, q_scale, tiles_per_seq):
    i = pl.program_id(0)
    j = pl.program_id(1)
    acc = jnp.dot(h_ref[...], w_ref[...], preferred_element_type=F32)
    _depthwise_chunk(hg_ref, halo_ref, wdw_ref, bdw_ref, cv_ref, hp_ref, i % tiles_per_seq == 0, i < 0)

    @pl.when(j < group_tiles)
    def _():
        o_ref[...] = (_head_rms(acc, qg_ref[...]) * q_scale).astype(BF16)

    @pl.when((j >= group_tiles) & (j < 2 * group_tiles))
    def _():
        o_ref[...] = _head_rms(acc, kg_ref[...]).astype(BF16)

    @pl.when((j >= 2 * group_tiles) & (j < 3 * group_tiles))
    def _():
        o_ref[...] = acc.astype(BF16)

    @pl.when(j >= 3 * group_tiles)
    def _():
        o_ref[...] = _silu(acc).astype(BF16)


def _in_proj(h, w_bf, q_norm_g, k_norm_g, hglu, w_dw, b_dw, width, seq):
    T, D = h.shape
    gt = width // PROJ_TN
    heads_per_tile = PROJ_TN // HEAD_DIM
    qg = jnp.tile(q_norm_g.reshape(1, HEAD_DIM), (1, heads_per_tile))
    kg = jnp.tile(k_norm_g.reshape(1, HEAD_DIM), (1, heads_per_tile))
    q_scale = (HEAD_DIM ** -0.5) * LOG2E
    chunks = width // LANES
    assert chunks == 4 * gt and seq % PROJ_TM == 0
    halo_per_tile = PROJ_TM // HALO_ROWS
    wdw_b = jnp.broadcast_to(w_dw.reshape(CONV_KERNEL, chunks, 1, LANES).transpose(1, 0, 2, 3),
                             (chunks, CONV_KERNEL, SUBLANES, LANES))
    bdw_b = jnp.broadcast_to(b_dw.reshape(chunks, 1, LANES), (chunks, SUBLANES, LANES))

    return pl.pallas_call(
        functools.partial(_inproj_kernel, group_tiles=gt, q_scale=q_scale, tiles_per_seq=seq // PROJ_TM),
        grid=(T // PROJ_TM, 4 * gt),
        in_specs=[pl.BlockSpec((PROJ_TM, D), lambda i, j: (i, 0)),
                  pl.BlockSpec((D, PROJ_TN), lambda i, j: (0, j)),
                  pl.BlockSpec((1, PROJ_TN), lambda i, j: (0, 0)),
                  pl.BlockSpec((1, PROJ_TN), lambda i, j: (0, 0)),
                  pl.BlockSpec((PROJ_TM, LANES), lambda i, j: (i, j)),
                  pl.BlockSpec((HALO_ROWS, LANES), lambda i, j: (jnp.maximum(i * halo_per_tile - 1, 0), j)),
                  pl.BlockSpec((1, CONV_KERNEL, SUBLANES, LANES), lambda i, j: (j, 0, 0, 0)),
                  pl.BlockSpec((1, SUBLANES, LANES), lambda i, j: (j, 0, 0))],
        out_specs=[pl.BlockSpec((PROJ_TM, PROJ_TN), lambda i, j: (i, j)),
                   pl.BlockSpec((PROJ_TM, LANES), lambda i, j: (i, j))],
        out_shape=[jax.ShapeDtypeStruct((T, 4 * width), BF16), jax.ShapeDtypeStruct((T, width), F32)],
        scratch_shapes=[pltpu.VMEM((HALO_ROWS + PROJ_TM, LANES), F32)],
        compiler_params=_params("arbitrary", "arbitrary"),
        name="in_proj",
    )(h, w_bf, qg, kg, hglu, hglu, wdw_b, bdw_b)


def _glu_kernel(h_ref, wu_ref, wg_ref, wc_ref, wsrc_ref, o_ref, sgc_ref, wcast_ref):
    h = h_ref[...]
    u = jnp.dot(h, wu_ref[...], preferred_element_type=F32)
    g = jnp.dot(h, wg_ref[...], preferred_element_type=F32)
    gc = jnp.dot(h, wc_ref[...], preferred_element_type=F32)
    o_ref[...] = (u * _sigmoid(g)).astype(BF16)
    sgc_ref[...] = _silu(gc).astype(BF16)
    wcast_ref[...] = wsrc_ref[...].astype(BF16)


def _glu_proj(h, w_conv_bf, w_in, width):
    T, D = h.shape
    tn = GLU_TN
    gt = width // tn
    rows, cols = T // PROJ_TM, gt
    cast_cols = 4 * width // tn
    assert (D // CAST_ROWS) * cast_cols == rows * cols
    cast_map = lambda i, j: ((i * cols + j) // cast_cols, (i * cols + j) % cast_cols)
    return pl.pallas_call(
        _glu_kernel,
        grid=(rows, cols),
        in_specs=[pl.BlockSpec((PROJ_TM, D), lambda i, j: (i, 0)),
                  pl.BlockSpec((D, tn), lambda i, j: (0, j)),
                  pl.BlockSpec((D, tn), lambda i, j: (0, gt + j)),
                  pl.BlockSpec((D, tn), lambda i, j: (0, 2 * gt + j)),
                  pl.BlockSpec((CAST_ROWS, tn), cast_map)],
        out_specs=[pl.BlockSpec((PROJ_TM, tn), lambda i, j: (i, j)),
                   pl.BlockSpec((PROJ_TM, tn), lambda i, j: (i, j)),
                   pl.BlockSpec((CAST_ROWS, tn), cast_map)],
        out_shape=[jax.ShapeDtypeStruct((T, width), BF16), jax.ShapeDtypeStruct((T, width), BF16),
                   jax.ShapeDtypeStruct((D, 4 * width), BF16)],
        compiler_params=_params("arbitrary", "arbitrary"),
        name="glu_proj",
    )(h, w_conv_bf, w_conv_bf, w_conv_bf, w_in)


def _attn_kernel(q_ref, k_ref, v_ref, sg_ref, tri_ref, o_ref, acc_ref, r_ref, z_ref, c_ref, w_ref):
    qi = pl.program_id(2)
    tq, tk = ATTN_TQ, ATTN_TK
    sub = tq // tk
    tri = tri_ref[...]

    def scores(rows, k0):
        return lax.dot_general(q_ref[rows, :], k_ref[pl.ds(k0, tk), :], (((1,), (1,)), ((), ())),
                               preferred_element_type=F32)

    def cost(z):
        return jnp.where(z > SOFTPLUS_LINEAR_ABOVE, z, jnp.log(1.0 + jnp.exp2(z)) * LOG2E)

    def causal(shape):
        return lax.broadcasted_iota(jnp.int32, shape, 1) < lax.broadcasted_iota(jnp.int32, shape, 0)

    def stage_scores(tile, slot):
        row0, k0, masked = tile
        rows = slice(row0, tq)
        z = scores(rows, k0)
        c = cost(z)
        if masked:
            c = jnp.where(causal(z.shape), c, 0.0)
        z_ref[slot, rows, :] = z
        c_ref[slot, rows, :] = c.astype(BF16)

    def stage_weights(tile, slot):
        row0, _, masked = tile
        rows = slice(row0, tq)
        cs = jnp.dot(c_ref[slot, rows, :], tri, preferred_element_type=F32)
        r = r_ref[rows, :]
        w = jnp.exp2(z_ref[slot, rows, :] - cs - r)
        if masked:
            w = jnp.where(causal(w.shape), w, 0.0)
        w_ref[slot, rows, :] = w.astype(BF16)
        r_ref[rows, :] = r + cs[:, 0:1]

    def stage_values(tile, slot):
        row0, k0, _ = tile
        rows = slice(row0, tq)
        acc_ref[rows, :] += jnp.dot(w_ref[slot, rows, :], v_ref[pl.ds(k0, tk), :], preferred_element_type=F32)

    def pipeline_steps(tiles, steps):
        for s in steps:
            if 0 <= s < len(tiles):
                stage_scores(tiles[s], s % 2)
            if 0 <= s - 1 < len(tiles):
                stage_weights(tiles[s - 1], (s - 1) % 2)
            if 0 <= s - 2 < len(tiles):
                stage_values(tiles[s - 2], (s - 2) % 2)

    acc_ref[...] = jnp.zeros_like(acc_ref)
    r_ref[...] = jnp.zeros_like(r_ref)
    diag = [(d * tk, pl.multiple_of(qi * tq + d * tk, tk), True) for d in reversed(range(sub))]

    @pl.when(qi == 0)
    def _():
        pipeline_steps(diag, range(sub + 2))

    @pl.when(qi > 0)
    def _():
        n = qi * sub
        early = lambda j: (0, pl.multiple_of((n - 1 - j) * tk, tk), False)
        pipeline_steps(diag + [early(0), early(1)], range(sub + 2))

        def body(p, carry):
            j = 2 * p
            stage_scores(early(j + 2), 0)
            stage_weights(early(j + 1), 1)
            stage_values(early(j), 0)
            stage_scores(early(j + 3), 1)
            stage_weights(early(j + 2), 0)
            stage_values(early(j + 1), 1)
            return carry

        lax.fori_loop(0, n // 2 - 1, body, 0)
        stage_weights(early(n - 1), 1)
        stage_values(early(n - 2), 0)
        stage_values(early(n - 1), 1)

    o_ref[...] = (acc_ref[...] * sg_ref[...].astype(F32)).astype(BF16)


def _sb_attention(p3, width):
    B, S, _ = p3.shape
    H = width // HEAD_DIM
    tq, tk = ATTN_TQ, ATTN_TK
    idx = jnp.arange(tk, dtype=jnp.int32)
    tri = (idx[:, None] >= idx[None, :]).astype(BF16)
    return pl.pallas_call(
        _attn_kernel,
        grid=(B, H, S // tq),
        in_specs=[pl.BlockSpec((None, tq, HEAD_DIM), lambda b, h, i: (b, i, h)),
                  pl.BlockSpec((None, S, HEAD_DIM), lambda b, h, i: (b, 0, H + h)),
                  pl.BlockSpec((None, S, HEAD_DIM), lambda b, h, i: (b, 0, 2 * H + h)),
                  pl.BlockSpec((None, tq, HEAD_DIM), lambda b, h, i: (b, i, 3 * H + h)),
                  pl.BlockSpec(tri.shape, lambda b, h, i: (0, 0))],
        out_specs=pl.BlockSpec((None, tq, HEAD_DIM), lambda b, h, i: (b, i, h)),
        out_shape=jax.ShapeDtypeStruct((B, S, width), BF16),
        scratch_shapes=[pltpu.VMEM((tq, HEAD_DIM), F32), pltpu.VMEM((tq, 1), F32),
                        pltpu.VMEM((2, tq, tk), F32), pltpu.VMEM((2, tq, tk), BF16),
                        pltpu.VMEM((2, tq, tk), BF16)],
        compiler_params=_params("arbitrary", "arbitrary", "arbitrary"),
        name="sb_attn",
    )(p3, p3, p3, p3, tri)


def _conv_kernel(cv_ref, sgc_ref, lng_ref, lnb_ref, wpw_ref, bpw_ref, o_ref):
    cv = cv_ref[...]
    mu = jnp.mean(cv, axis=-1, keepdims=True)
    xc = cv - mu
    var = jnp.mean(xc * xc, axis=-1, keepdims=True)
    yn = xc * lax.rsqrt(var + EPS) * lng_ref[...] + lnb_ref[...]
    y = jnp.dot(_silu(yn).astype(BF16), wpw_ref[...], preferred_element_type=F32) + bpw_ref[...]
    o_ref[...] = (y * sgc_ref[...].astype(F32)).astype(BF16)


def _conv_mix(cv, sgc, ln_g, ln_b, w_pw_bf, b_pw):
    T, C = cv.shape
    ts = CONV_TS
    vec = lambda a: a.reshape(1, C)
    const2 = lambda s: (0, 0)
    return pl.pallas_call(
        _conv_kernel,
        grid=(T // ts,),
        in_specs=[pl.BlockSpec((ts, C), lambda s: (s, 0)),
                  pl.BlockSpec((ts, C), lambda s: (s, 0)),
                  pl.BlockSpec((1, C), const2), pl.BlockSpec((1, C), const2),
                  pl.BlockSpec((C, C), const2), pl.BlockSpec((1, C), const2)],
        out_specs=pl.BlockSpec((ts, C), lambda s: (s, 0)),
        out_shape=jax.ShapeDtypeStruct((T, C), BF16),
        compiler_params=_params("arbitrary"),
        name="conv_mix",
    )(cv, sgc, vec(ln_g), vec(ln_b), w_pw_bf, vec(b_pw))


def _outproj_kernel(ysb_ref, ycv_ref, wt_ref, wb_ref, x_ref, gate_ref, o_ref):
    y = (jnp.dot(ysb_ref[...], wt_ref[...], preferred_element_type=F32)
         + jnp.dot(ycv_ref[...], wb_ref[...], preferred_element_type=F32))
    o_ref[...] = x_ref[...] + gate_ref[0] * y


def _out_proj(y_sb, y_conv, w_out_bf, x2, mod3, seq):
    T, W = y_sb.shape
    D = x2.shape[1]
    tiles_per_batch = seq // OUT_TM
    return pl.pallas_call(
        _outproj_kernel,
        grid=(T // OUT_TM, D // OUT_TN),
        in_specs=[pl.BlockSpec((OUT_TM, W), lambda i, j: (i, 0)),
                  pl.BlockSpec((OUT_TM, W), lambda i, j: (i, 0)),
                  pl.BlockSpec((W, OUT_TN), lambda i, j: (0, j)),
                  pl.BlockSpec((W, OUT_TN), lambda i, j: (1, j)),
                  pl.BlockSpec((OUT_TM, OUT_TN), lambda i, j: (i, j)),
                  pl.BlockSpec((1, 1, OUT_TN), lambda i, j: (3 * (i // tiles_per_batch) + 2, 0, j))],
        out_specs=pl.BlockSpec((OUT_TM, OUT_TN), lambda i, j: (i, j)),
        out_shape=jax.ShapeDtypeStruct((T, D), F32),
        compiler_params=_params("arbitrary", "arbitrary"),
        name="out_proj",
    )(y_sb, y_conv, w_out_bf, w_out_bf, x2, mod3)


def _layer(x, c, norm_g, w_ada, b_ada, w_in, q_norm_g, k_norm_g, w_dw, b_dw, ln_g, ln_b, w_pw, b_pw, w_out):
    B, S, D = x.shape
    width = w_pw.shape[0]
    assert w_in.shape[1] == 7 * width and w_out.shape[0] == 2 * width
    x2 = x.reshape(B * S, D)

    mod = _adaln_mod(c, w_ada, b_ada)
    mod3 = mod.reshape(B * 3, 1, D)
    h = _norm_mod(x2, norm_g, mod3, S)

    hglu, sgc, w_attn_bf = _glu_proj(h, w_in[:, 4 * width:].astype(BF16), w_in, width)
    p2, cv = _in_proj(h, w_attn_bf, q_norm_g, k_norm_g, hglu, w_dw, b_dw, width, S)

    y_sb = _sb_attention(p2.reshape(B, S, 4 * width), width).reshape(B * S, width)
    y_conv = _conv_mix(cv, sgc, ln_g, ln_b, w_pw.astype(BF16), b_pw)

    out = _out_proj(y_sb, y_conv, w_out.astype(BF16), x2, mod3, S)
    return out.reshape(B, S, D)


def kernel(x, c, norm_g, w_ada, b_ada, w_in, q_norm_g, k_norm_g, w_dw, b_dw, ln_g, ln_b, w_pw, b_pw, w_out):
    for layer in range(norm_g.shape[0]):
        x = _layer(x, c, norm_g[layer], w_ada[layer], b_ada[layer], w_in[layer], q_norm_g[layer],
                   k_norm_g[layer], w_dw[layer], b_dw[layer], ln_g[layer], ln_b[layer], w_pw[layer],
                   b_pw[layer], w_out[layer])
    return x
```

```python
import functools
import math

import jax
import jax.numpy as jnp
from jax import lax
from jax.experimental import pallas as pl
from jax.experimental.pallas import tpu as pltpu

F32 = jnp.float32
BF16 = jnp.bfloat16

HEAD_DIM = 128
CONV_KERNEL = 31
EPS = 1e-6
LOG2E = math.log2(math.e)

LANES = 128
SUBLANES = 8
HALO_ROWS = 32
VMEM_LIMIT = 56 * 1024 * 1024

NORM_TM = 512
PROJ_TM, PROJ_TN = 1024, 512
GLU_TN = 256
CAST_ROWS = 1024
CAST_COLS = 256
ATTN_TQ, ATTN_TK = 1024, 256
assert (ATTN_TQ // ATTN_TK) % 2 == 0
SOFTPLUS_LINEAR_ABOVE = 64.0
CONV_TS = 512
CONV_CHAINS = 2
OUT_TM, OUT_TN = 1024, 512
ADA_TN = 512


def _params(*sem):
    return pltpu.CompilerParams(dimension_semantics=sem, vmem_limit_bytes=VMEM_LIMIT)


def _sigmoid(x):
    return 0.5 * jnp.tanh(0.5 * x) + 0.5


def _silu(x):
    return x * _sigmoid(x)


def _adaln_kernel(c_ref, w_ref, b_ref, o_ref):
    a = _silu(c_ref[...]).astype(BF16)
    o_ref[...] = jnp.dot(a, w_ref[...].astype(BF16), preferred_element_type=F32) + b_ref[...]


def _adaln_mod(c, w_ada, b_ada):
    B, D = c.shape
    N = w_ada.shape[1]
    return pl.pallas_call(
        _adaln_kernel,
        grid=(N // ADA_TN,),
        in_specs=[pl.BlockSpec((B, D), lambda j: (0, 0)),
                  pl.BlockSpec((D, ADA_TN), lambda j: (0, j)),
                  pl.BlockSpec((1, ADA_TN), lambda j: (0, j))],
        out_specs=pl.BlockSpec((B, ADA_TN), lambda j: (0, j)),
        out_shape=jax.ShapeDtypeStruct((B, N), F32),
        compiler_params=_params("arbitrary"),
        name="adaln_mod",
    )(c, w_ada, b_ada.reshape(1, N))


def _norm_kernel(x_ref, g_ref, shift_ref, scale_ref, o_ref):
    x = x_ref[...]
    ms = jnp.mean(x * x, axis=-1, keepdims=True)
    y = x * lax.rsqrt(ms + EPS) * g_ref[...]
    o_ref[...] = (y * (1.0 + scale_ref[0]) + shift_ref[0]).astype(BF16)


def _norm_mod(x2, norm_g, mod3, seq):
    T, D = x2.shape
    tiles_per_batch = seq // NORM_TM
    return pl.pallas_call(
        _norm_kernel,
        grid=(T // NORM_TM,),
        in_specs=[pl.BlockSpec((NORM_TM, D), lambda i: (i, 0)),
                  pl.BlockSpec((1, D), lambda i: (0, 0)),
                  pl.BlockSpec((1, 1, D), lambda i: (3 * (i // tiles_per_batch), 0, 0)),
                  pl.BlockSpec((1, 1, D), lambda i: (3 * (i // tiles_per_batch) + 1, 0, 0))],
        out_specs=pl.BlockSpec((NORM_TM, D), lambda i: (i, 0)),
        out_shape=jax.ShapeDtypeStruct((T, D), BF16),
        compiler_params=_params("arbitrary"),
        name="norm_mod",
    )(x2, norm_g.reshape(1, D), mod3, mod3)


def _head_rms(a, g):
    parts = []
    for hh in range(a.shape[1] // HEAD_DIM):
        c = a[:, hh * HEAD_DIM:(hh + 1) * HEAD_DIM]
        ms = jnp.mean(c * c, axis=-1, keepdims=True)
        parts.append(c * lax.rsqrt(ms + EPS))
    return jnp.concatenate(parts, axis=-1) * g


def _depthwise_chunk(hg_ref, halo_ref, wdw_ref, bdw_ref, cv_ref, hp_ref, sequence_start, never):
    rows = hg_ref.shape[0]
    first_tap_row = HALO_ROWS - (CONV_KERNEL - 1)
    hp_ref[0:HALO_ROWS, :] = jnp.where(sequence_start, 0.0, halo_ref[...].astype(F32))
    hp_ref[HALO_ROWS:, :] = hg_ref[...].astype(F32)
    bias = bdw_ref[0]
    taps = [wdw_ref[0, k] for k in range(CONV_KERNEL)]
    prev = [bias] * CONV_CHAINS
    for g in range(rows // SUBLANES):
        acc = jnp.where(never, prev[g % CONV_CHAINS], bias)
        for k in range(CONV_KERNEL):
            start = SUBLANES * g + first_tap_row + k
            acc = acc + hp_ref[start:start + SUBLANES, :] * taps[k]
        cv_ref[SUBLANES * g:SUBLANES * (g + 1), :] = acc
        prev[g % CONV_CHAINS] = acc


def _inproj_kernel(h_ref, w_ref, qg_ref, kg_ref, hg_ref, halo_ref, wdw_ref, bdw_ref, wo_ref, wp_ref,
                   o_ref, cv_ref, wo_cast_ref, wp_cast_ref, hp_ref, *, group_tiles, q_scale, tiles_per_seq):
    i = pl.program_id(0)
    j = pl.program_id(1)
    wo_cast_ref[...] = wo_ref[...].astype(BF16)
    wp_cast_ref[...] = wp_ref[...].astype(BF16)
    acc = jnp.dot(h_ref[...], w_ref[...], preferred_element_type=F32)
    _depthwise_chunk(hg_ref, halo_ref, wdw_ref, bdw_ref, cv_ref, hp_ref, i % tiles_per_seq == 0, i < 0)

    @pl.when(j < group_tiles)
    def _():
        o_ref[...] = (_head_rms(acc, qg_ref[...]) * q_scale).astype(BF16)

    @pl.when((j >= group_tiles) & (j < 2 * group_tiles))
    def _():
        o_ref[...] = _head_rms(acc, kg_ref[...]).astype(BF16)

    @pl.when((j >= 2 * group_tiles) & (j < 3 * group_tiles))
    def _():
        o_ref[...] = acc.astype(BF16)

    @pl.when(j >= 3 * group_tiles)
    def _():
        o_ref[...] = _silu(acc).astype(BF16)


def _in_proj(h, w_bf, q_norm_g, k_norm_g, hglu, w_dw, b_dw, w_out, w_pw, width, seq):
    T, D = h.shape
    gt = width // PROJ_TN
    heads_per_tile = PROJ_TN // HEAD_DIM
    qg = jnp.tile(q_norm_g.reshape(1, HEAD_DIM), (1, heads_per_tile))
    kg = jnp.tile(k_norm_g.reshape(1, HEAD_DIM), (1, heads_per_tile))
    q_scale = (HEAD_DIM ** -0.5) * LOG2E
    chunks = width // LANES
    assert chunks == 4 * gt and seq % PROJ_TM == 0
    halo_per_tile = PROJ_TM // HALO_ROWS
    wdw_b = jnp.broadcast_to(w_dw.reshape(CONV_KERNEL, chunks, 1, LANES).transpose(1, 0, 2, 3),
                             (chunks, CONV_KERNEL, SUBLANES, LANES))
    bdw_b = jnp.broadcast_to(b_dw.reshape(chunks, 1, LANES), (chunks, SUBLANES, LANES))

    steps = (T // PROJ_TM) * 4 * gt

    def cast_spec(w):
        col_blocks = w.shape[1] // CAST_COLS
        block_rows = w.shape[0] * col_blocks // steps
        assert block_rows % (2 * SUBLANES) == 0 and block_rows * steps == w.shape[0] * col_blocks
        return pl.BlockSpec((block_rows, CAST_COLS),
                            lambda i, j: ((i * (4 * gt) + j) // col_blocks, (i * (4 * gt) + j) % col_blocks))

    return pl.pallas_call(
        functools.partial(_inproj_kernel, group_tiles=gt, q_scale=q_scale, tiles_per_seq=seq // PROJ_TM),
        grid=(T // PROJ_TM, 4 * gt),
        in_specs=[pl.BlockSpec((PROJ_TM, D), lambda i, j: (i, 0)),
                  pl.BlockSpec((D, PROJ_TN), lambda i, j: (0, j)),
                  pl.BlockSpec((1, PROJ_TN), lambda i, j: (0, 0)),
                  pl.BlockSpec((1, PROJ_TN), lambda i, j: (0, 0)),
                  pl.BlockSpec((PROJ_TM, LANES), lambda i, j: (i, j)),
                  pl.BlockSpec((HALO_ROWS, LANES), lambda i, j: (jnp.maximum(i * halo_per_tile - 1, 0), j)),
                  pl.BlockSpec((1, CONV_KERNEL, SUBLANES, LANES), lambda i, j: (j, 0, 0, 0)),
                  pl.BlockSpec((1, SUBLANES, LANES), lambda i, j: (j, 0, 0)),
                  cast_spec(w_out),
                  cast_spec(w_pw)],
        out_specs=[pl.BlockSpec((PROJ_TM, PROJ_TN), lambda i, j: (i, j)),
                   pl.BlockSpec((PROJ_TM, LANES), lambda i, j: (i, j)),
                   cast_spec(w_out),
                   cast_spec(w_pw)],
        out_shape=[jax.ShapeDtypeStruct((T, 4 * width), BF16), jax.ShapeDtypeStruct((T, width), F32),
                   jax.ShapeDtypeStruct(w_out.shape, BF16), jax.ShapeDtypeStruct(w_pw.shape, BF16)],
        scratch_shapes=[pltpu.VMEM((HALO_ROWS + PROJ_TM, LANES), F32)],
        compiler_params=_params("arbitrary", "arbitrary"),
        name="in_proj",
    )(h, w_bf, qg, kg, hglu, hglu, wdw_b, bdw_b, w_out, w_pw)


def _glu_kernel(h_ref, wu_ref, wg_ref, wc_ref, wsrc_ref, o_ref, sgc_ref, wcast_ref):
    h = h_ref[...]
    u = jnp.dot(h, wu_ref[...], preferred_element_type=F32)
    g = jnp.dot(h, wg_ref[...], preferred_element_type=F32)
    gc = jnp.dot(h, wc_ref[...], preferred_element_type=F32)
    o_ref[...] = (u * _sigmoid(g)).astype(BF16)
    sgc_ref[...] = _silu(gc).astype(BF16)
    wcast_ref[...] = wsrc_ref[...].astype(BF16)


def _glu_proj(h, w_conv_bf, w_in, width):
    T, D = h.shape
    tn = GLU_TN
    gt = width // tn
    rows, cols = T // PROJ_TM, gt
    cast_cols = 4 * width // tn
    assert (D // CAST_ROWS) * cast_cols == rows * cols
    cast_map = lambda i, j: ((i * cols + j) // cast_cols, (i * cols + j) % cast_cols)
    return pl.pallas_call(
        _glu_kernel,
        grid=(rows, cols),
        in_specs=[pl.BlockSpec((PROJ_TM, D), lambda i, j: (i, 0)),
                  pl.BlockSpec((D, tn), lambda i, j: (0, j)),
                  pl.BlockSpec((D, tn), lambda i, j: (0, gt + j)),
                  pl.BlockSpec((D, tn), lambda i, j: (0, 2 * gt + j)),
                  pl.BlockSpec((CAST_ROWS, tn), cast_map)],
        out_specs=[pl.BlockSpec((PROJ_TM, tn), lambda i, j: (i, j)),
                   pl.BlockSpec((PROJ_TM, tn), lambda i, j: (i, j)),
                   pl.BlockSpec((CAST_ROWS, tn), cast_map)],
        out_shape=[jax.ShapeDtypeStruct((T, width), BF16), jax.ShapeDtypeStruct((T, width), BF16),
                   jax.ShapeDtypeStruct((D, 4 * width), BF16)],
        compiler_params=_params("arbitrary", "arbitrary"),
        name="glu_proj",
    )(h, w_conv_bf, w_conv_bf, w_conv_bf, w_in)


def _attn_kernel(q_ref, k_ref, v_ref, sg_ref, tri_ref, o_ref, acc_ref, r_ref, z_ref, c_ref, w_ref):
    tq, tk = ATTN_TQ, ATTN_TK
    sub = tq // tk
    tri = tri_ref[...]

    def cost(z):
        return jnp.where(z > SOFTPLUS_LINEAR_ABOVE, z, jnp.log(1.0 + jnp.exp2(z)) * LOG2E)

    def causal(shape):
        return lax.broadcasted_iota(jnp.int32, shape, 1) < lax.broadcasted_iota(jnp.int32, shape, 0)

    def query_tile(qi):
        q0 = qi * tq

        def stage_scores(tile, slot):
            row0, k0, masked = tile
            rows = slice(row0, tq)
            z = lax.dot_general(q_ref[q0 + row0:q0 + tq, :], k_ref[pl.ds(k0, tk), :], (((1,), (1,)), ((), ())),
                                preferred_element_type=F32)
            c = cost(z)
            if masked:
                c = jnp.where(causal(z.shape), c, 0.0)
            z_ref[slot, rows, :] = z
            c_ref[slot, rows, :] = c.astype(BF16)

        def stage_weights(tile, slot):
            row0, _, masked = tile
            rows = slice(row0, tq)
            cs = jnp.dot(c_ref[slot, rows, :], tri, preferred_element_type=F32)
            r = r_ref[rows, :]
            w = jnp.exp2(z_ref[slot, rows, :] - cs - r)
            if masked:
                w = jnp.where(causal(w.shape), w, 0.0)
            w_ref[slot, rows, :] = w.astype(BF16)
            r_ref[rows, :] = r + cs[:, 0:1]

        def stage_values(tile, slot):
            row0, k0, _ = tile
            rows = slice(row0, tq)
            acc_ref[rows, :] += jnp.dot(w_ref[slot, rows, :], v_ref[pl.ds(k0, tk), :], preferred_element_type=F32)

        def pipeline_steps(tiles, steps):
            for s in steps:
                if 0 <= s < len(tiles):
                    stage_scores(tiles[s], s % 2)
                if 0 <= s - 1 < len(tiles):
                    stage_weights(tiles[s - 1], (s - 1) % 2)
                if 0 <= s - 2 < len(tiles):
                    stage_values(tiles[s - 2], (s - 2) % 2)

        acc_ref[...] = jnp.zeros_like(acc_ref)
        r_ref[...] = jnp.zeros_like(r_ref)
        diag = [(d * tk, q0 + d * tk, True) for d in reversed(range(sub))]
        n = qi * sub
        early = lambda j: (0, (n - 1 - j) * tk, False)
        if n == 0:
            pipeline_steps(diag, range(sub + 2))
        else:
            pipeline_steps(diag + [early(0), early(1)], range(sub + 2))

            def body(p, carry):
                j = 2 * p
                dyn = lambda jj: (0, pl.multiple_of((n - 1 - jj) * tk, tk), False)
                stage_scores(dyn(j + 2), 0)
                stage_weights(dyn(j + 1), 1)
                stage_values(dyn(j), 0)
                stage_scores(dyn(j + 3), 1)
                stage_weights(dyn(j + 2), 0)
                stage_values(dyn(j + 1), 1)
                return carry

            lax.fori_loop(0, n // 2 - 1, body, 0)
            stage_weights(early(n - 1), 1)
            stage_values(early(n - 2), 0)
            stage_values(early(n - 1), 1)

        o_ref[q0:q0 + tq, :] = (acc_ref[...] * sg_ref[q0:q0 + tq, :].astype(F32)).astype(BF16)

    for qi in range(q_ref.shape[0] // tq):
        query_tile(qi)


def _sb_attention(p3, width):
    B, S, _ = p3.shape
    H = width // HEAD_DIM
    tq, tk = ATTN_TQ, ATTN_TK
    assert S % tq == 0
    idx = jnp.arange(tk, dtype=jnp.int32)
    tri = (idx[:, None] >= idx[None, :]).astype(BF16)
    head_block = lambda group: pl.BlockSpec((None, S, HEAD_DIM), lambda b, h: (b, 0, group * H + h))
    return pl.pallas_call(
        _attn_kernel,
        grid=(B, H),
        in_specs=[head_block(0), head_block(1), head_block(2), head_block(3),
                  pl.BlockSpec(tri.shape, lambda b, h: (0, 0))],
        out_specs=head_block(0),
        out_shape=jax.ShapeDtypeStruct((B, S, width), BF16),
        scratch_shapes=[pltpu.VMEM((tq, HEAD_DIM), F32), pltpu.VMEM((tq, 1), F32),
                        pltpu.VMEM((2, tq, tk), F32), pltpu.VMEM((2, tq, tk), BF16),
                        pltpu.VMEM((2, tq, tk), BF16)],
        compiler_params=_params("arbitrary", "arbitrary"),
        name="sb_attn",
    )(p3, p3, p3, p3, tri)


def _conv_kernel(cv_ref, sgc_ref, lng_ref, lnb_ref, wpw_ref, bpw_ref, o_ref):
    cv = cv_ref[...]
    mu = jnp.mean(cv, axis=-1, keepdims=True)
    xc = cv - mu
    var = jnp.mean(xc * xc, axis=-1, keepdims=True)
    yn = xc * lax.rsqrt(var + EPS) * lng_ref[...] + lnb_ref[...]
    y = jnp.dot(_silu(yn).astype(BF16), wpw_ref[...], preferred_element_type=F32) + bpw_ref[...]
    o_ref[...] = (y * sgc_ref[...].astype(F32)).astype(BF16)


def _conv_mix(cv, sgc, ln_g, ln_b, w_pw_bf, b_pw):
    T, C = cv.shape
    ts = CONV_TS
    vec = lambda a: a.reshape(1, C)
    const2 = lambda s: (0, 0)
    return pl.pallas_call(
        _conv_kernel,
        grid=(T // ts,),
        in_specs=[pl.BlockSpec((ts, C), lambda s: (s, 0)),
                  pl.BlockSpec((ts, C), lambda s: (s, 0)),
                  pl.BlockSpec((1, C), const2), pl.BlockSpec((1, C), const2),
                  pl.BlockSpec((C, C), const2), pl.BlockSpec((1, C), const2)],
        out_specs=pl.BlockSpec((ts, C), lambda s: (s, 0)),
        out_shape=jax.ShapeDtypeStruct((T, C), BF16),
        compiler_params=_params("arbitrary"),
        name="conv_mix",
    )(cv, sgc, vec(ln_g), vec(ln_b), w_pw_bf, vec(b_pw))


def _outproj_kernel(ysb_ref, ycv_ref, wt_ref, wb_ref, x_ref, gate_ref, o_ref):
    y = (jnp.dot(ysb_ref[...], wt_ref[...], preferred_element_type=F32)
         + jnp.dot(ycv_ref[...], wb_ref[...], preferred_element_type=F32))
    o_ref[...] = x_ref[...] + gate_ref[0] * y


def _out_proj(y_sb, y_conv, w_out_bf, x2, mod3, seq):
    T, W = y_sb.shape
    D = x2.shape[1]
    tiles_per_batch = seq // OUT_TM
    return pl.pallas_call(
        _outproj_kernel,
        grid=(T // OUT_TM, D // OUT_TN),
        in_specs=[pl.BlockSpec((OUT_TM, W), lambda i, j: (i, 0)),
                  pl.BlockSpec((OUT_TM, W), lambda i, j: (i, 0)),
                  pl.BlockSpec((W, OUT_TN), lambda i, j: (0, j)),
                  pl.BlockSpec((W, OUT_TN), lambda i, j: (1, j)),
                  pl.BlockSpec((OUT_TM, OUT_TN), lambda i, j: (i, j)),
                  pl.BlockSpec((1, 1, OUT_TN), lambda i, j: (3 * (i // tiles_per_batch) + 2, 0, j))],
        out_specs=pl.BlockSpec((OUT_TM, OUT_TN), lambda i, j: (i, j)),
        out_shape=jax.ShapeDtypeStruct((T, D), F32),
        compiler_params=_params("arbitrary", "arbitrary"),
        name="out_proj",
    )(y_sb, y_conv, w_out_bf, w_out_bf, x2, mod3)


def _layer(x, c, norm_g, w_ada, b_ada, w_in, q_norm_g, k_norm_g, w_dw, b_dw, ln_g, ln_b, w_pw, b_pw, w_out):
    B, S, D = x.shape
    width = w_pw.shape[0]
    assert w_in.shape[1] == 7 * width and w_out.shape[0] == 2 * width
    x2 = x.reshape(B * S, D)

    mod = _adaln_mod(c, w_ada, b_ada)
    mod3 = mod.reshape(B * 3, 1, D)
    h = _norm_mod(x2, norm_g, mod3, S)

    hglu, sgc, w_attn_bf = _glu_proj(h, w_in[:, 4 * width:].astype(BF16), w_in, width)
    p2, cv, w_out_bf, w_pw_bf = _in_proj(h, w_attn_bf, q_norm_g, k_norm_g, hglu, w_dw, b_dw, w_out, w_pw, width, S)

    y_sb = _sb_attention(p2.reshape(B, S, 4 * width), width).reshape(B * S, width)
    y_conv = _conv_mix(cv, sgc, ln_g, ln_b, w_pw_bf, b_pw)

    out = _out_proj(y_sb, y_conv, w_out_bf, x2, mod3, S)
    return out.reshape(B, S, D)


def kernel(x, c, norm_g, w_ada, b_ada, w_in, q_norm_g, k_norm_g, w_dw, b_dw, ln_g, ln_b, w_pw, b_pw, w_out):
    for layer in range(norm_g.shape[0]):
        x = _layer(x, c, norm_g[layer], w_ada[layer], b_ada[layer], w_in[layer], q_norm_g[layer],
                   k_norm_g[layer], w_dw[layer], b_dw[layer], ln_g[layer], ln_b[layer], w_pw[layer],
                   b_pw[layer], w_out[layer])
    return x
```

```python
import functools
import math

import jax
import jax.numpy as jnp
from jax import lax
from jax.experimental import pallas as pl
from jax.experimental.pallas import tpu as pltpu

F32 = jnp.float32
BF16 = jnp.bfloat16

HEAD_DIM = 128
CONV_KERNEL = 31
EPS = 1e-6
LOG2E = math.log2(math.e)

LANES = 128
SUBLANES = 8
HALO_ROWS = 32
VMEM_LIMIT = 56 * 1024 * 1024

NORM_TM = 512
PROJ_TM, PROJ_TN = 1024, 512
GLU_TN = 256
CAST_ROWS = 1024
CAST_COLS = 256
ATTN_TQ, ATTN_TK = 1024, 256
assert (ATTN_TQ // ATTN_TK) % 2 == 0
SOFTPLUS_LINEAR_ABOVE = 64.0
CONV_TS = 512
CONV_CHAINS = 2
OUT_TM, OUT_TN = 1024, 1024
ADA_TN = 512


def _params(*sem):
    return pltpu.CompilerParams(dimension_semantics=sem, vmem_limit_bytes=VMEM_LIMIT)


def _sigmoid(x):
    return 0.5 * jnp.tanh(0.5 * x) + 0.5


def _silu(x):
    return x * _sigmoid(x)


def _adaln_kernel(c_ref, w_ref, b_ref, o_ref):
    a = _silu(c_ref[...]).astype(BF16)
    o_ref[...] = jnp.dot(a, w_ref[...].astype(BF16), preferred_element_type=F32) + b_ref[...]


def _adaln_mod(c, w_ada, b_ada):
    B, D = c.shape
    N = w_ada.shape[1]
    return pl.pallas_call(
        _adaln_kernel,
        grid=(N // ADA_TN,),
        in_specs=[pl.BlockSpec((B, D), lambda j: (0, 0)),
                  pl.BlockSpec((D, ADA_TN), lambda j: (0, j)),
                  pl.BlockSpec((1, ADA_TN), lambda j: (0, j))],
        out_specs=pl.BlockSpec((B, ADA_TN), lambda j: (0, j)),
        out_shape=jax.ShapeDtypeStruct((B, N), F32),
        compiler_params=_params("arbitrary"),
        name="adaln_mod",
    )(c, w_ada, b_ada.reshape(1, N))


def _norm_kernel(x_ref, g_ref, shift_ref, scale_ref, o_ref):
    x = x_ref[...]
    ms = jnp.mean(x * x, axis=-1, keepdims=True)
    y = x * lax.rsqrt(ms + EPS) * g_ref[...]
    o_ref[...] = (y * (1.0 + scale_ref[0]) + shift_ref[0]).astype(BF16)


def _norm_mod(x2, norm_g, mod3, seq):
    T, D = x2.shape
    tiles_per_batch = seq // NORM_TM
    return pl.pallas_call(
        _norm_kernel,
        grid=(T // NORM_TM,),
        in_specs=[pl.BlockSpec((NORM_TM, D), lambda i: (i, 0)),
                  pl.BlockSpec((1, D), lambda i: (0, 0)),
                  pl.BlockSpec((1, 1, D), lambda i: (3 * (i // tiles_per_batch), 0, 0)),
                  pl.BlockSpec((1, 1, D), lambda i: (3 * (i // tiles_per_batch) + 1, 0, 0))],
        out_specs=pl.BlockSpec((NORM_TM, D), lambda i: (i, 0)),
        out_shape=jax.ShapeDtypeStruct((T, D), BF16),
        compiler_params=_params("arbitrary"),
        name="norm_mod",
    )(x2, norm_g.reshape(1, D), mod3, mod3)


def _head_rms(a, g):
    parts = []
    for hh in range(a.shape[1] // HEAD_DIM):
        c = a[:, hh * HEAD_DIM:(hh + 1) * HEAD_DIM]
        ms = jnp.mean(c * c, axis=-1, keepdims=True)
        parts.append(c * lax.rsqrt(ms + EPS))
    return jnp.concatenate(parts, axis=-1) * g


def _depthwise_chunk(hg_ref, halo_ref, wdw_ref, bdw_ref, cv_ref, hp_ref, sequence_start, never):
    rows = hg_ref.shape[0]
    first_tap_row = HALO_ROWS - (CONV_KERNEL - 1)
    hp_ref[0:HALO_ROWS, :] = jnp.where(sequence_start, 0.0, halo_ref[...].astype(F32))
    hp_ref[HALO_ROWS:, :] = hg_ref[...].astype(F32)
    bias = bdw_ref[0]
    taps = [wdw_ref[0, k] for k in range(CONV_KERNEL)]
    prev = [bias] * CONV_CHAINS
    for g in range(rows // SUBLANES):
        acc = jnp.where(never, prev[g % CONV_CHAINS], bias)
        for k in range(CONV_KERNEL):
            start = SUBLANES * g + first_tap_row + k
            acc = acc + hp_ref[start:start + SUBLANES, :] * taps[k]
        cv_ref[SUBLANES * g:SUBLANES * (g + 1), :] = acc
        prev[g % CONV_CHAINS] = acc


def _inproj_kernel(h_ref, w_ref, qg_ref, kg_ref, hg_ref, halo_ref, wdw_ref, bdw_ref, wo_ref, wp_ref,
                   o_ref, cv_ref, wo_cast_ref, wp_cast_ref, hp_ref, *, group_tiles, q_scale, tiles_per_seq):
    i = pl.program_id(0)
    j = pl.program_id(1)
    wo_cast_ref[...] = wo_ref[...].astype(BF16)
    wp_cast_ref[...] = wp_ref[...].astype(BF16)
    acc = jnp.dot(h_ref[...], w_ref[...], preferred_element_type=F32)
    _depthwise_chunk(hg_ref, halo_ref, wdw_ref, bdw_ref, cv_ref, hp_ref, i % tiles_per_seq == 0, i < 0)

    @pl.when(j < group_tiles)
    def _():
        o_ref[...] = (_head_rms(acc, qg_ref[...]) * q_scale).astype(BF16)

    @pl.when((j >= group_tiles) & (j < 2 * group_tiles))
    def _():
        o_ref[...] = _head_rms(acc, kg_ref[...]).astype(BF16)

    @pl.when((j >= 2 * group_tiles) & (j < 3 * group_tiles))
    def _():
        o_ref[...] = acc.astype(BF16)

    @pl.when(j >= 3 * group_tiles)
    def _():
        o_ref[...] = _silu(acc).astype(BF16)


def _in_proj(h, w_bf, q_norm_g, k_norm_g, hglu, w_dw, b_dw, w_out, w_pw, width, seq):
    T, D = h.shape
    gt = width // PROJ_TN
    heads_per_tile = PROJ_TN // HEAD_DIM
    qg = jnp.tile(q_norm_g.reshape(1, HEAD_DIM), (1, heads_per_tile))
    kg = jnp.tile(k_norm_g.reshape(1, HEAD_DIM), (1, heads_per_tile))
    q_scale = (HEAD_DIM ** -0.5) * LOG2E
    chunks = width // LANES
    assert chunks == 4 * gt and seq % PROJ_TM == 0
    halo_per_tile = PROJ_TM // HALO_ROWS
    wdw_b = jnp.broadcast_to(w_dw.reshape(CONV_KERNEL, chunks, 1, LANES).transpose(1, 0, 2, 3),
                             (chunks, CONV_KERNEL, SUBLANES, LANES))
    bdw_b = jnp.broadcast_to(b_dw.reshape(chunks, 1, LANES), (chunks, SUBLANES, LANES))

    steps = (T // PROJ_TM) * 4 * gt

    def cast_spec(w):
        col_blocks = w.shape[1] // CAST_COLS
        block_rows = w.shape[0] * col_blocks // steps
        assert block_rows % (2 * SUBLANES) == 0 and block_rows * steps == w.shape[0] * col_blocks
        return pl.BlockSpec((block_rows, CAST_COLS),
                            lambda i, j: ((i * (4 * gt) + j) // col_blocks, (i * (4 * gt) + j) % col_blocks))

    return pl.pallas_call(
        functools.partial(_inproj_kernel, group_tiles=gt, q_scale=q_scale, tiles_per_seq=seq // PROJ_TM),
        grid=(T // PROJ_TM, 4 * gt),
        in_specs=[pl.BlockSpec((PROJ_TM, D), lambda i, j: (i, 0)),
                  pl.BlockSpec((D, PROJ_TN), lambda i, j: (0, j)),
                  pl.BlockSpec((1, PROJ_TN), lambda i, j: (0, 0)),
                  pl.BlockSpec((1, PROJ_TN), lambda i, j: (0, 0)),
                  pl.BlockSpec((PROJ_TM, LANES), lambda i, j: (i, j)),
                  pl.BlockSpec((HALO_ROWS, LANES), lambda i, j: (jnp.maximum(i * halo_per_tile - 1, 0), j)),
                  pl.BlockSpec((1, CONV_KERNEL, SUBLANES, LANES), lambda i, j: (j, 0, 0, 0)),
                  pl.BlockSpec((1, SUBLANES, LANES), lambda i, j: (j, 0, 0)),
                  cast_spec(w_out),
                  cast_spec(w_pw)],
        out_specs=[pl.BlockSpec((PROJ_TM, PROJ_TN), lambda i, j: (i, j)),
                   pl.BlockSpec((PROJ_TM, LANES), lambda i, j: (i, j)),
                   cast_spec(w_out),
                   cast_spec(w_pw)],
        out_shape=[jax.ShapeDtypeStruct((T, 4 * width), BF16), jax.ShapeDtypeStruct((T, width), F32),
                   jax.ShapeDtypeStruct(w_out.shape, BF16), jax.ShapeDtypeStruct(w_pw.shape, BF16)],
        scratch_shapes=[pltpu.VMEM((HALO_ROWS + PROJ_TM, LANES), F32)],
        compiler_params=_params("arbitrary", "arbitrary"),
        name="in_proj",
    )(h, w_bf, qg, kg, hglu, hglu, wdw_b, bdw_b, w_out, w_pw)


def _glu_kernel(h_ref, wu_ref, wg_ref, wc_ref, wsrc_ref, o_ref, sgc_ref, wcast_ref):
    h = h_ref[...]
    u = jnp.dot(h, wu_ref[...], preferred_element_type=F32)
    g = jnp.dot(h, wg_ref[...], preferred_element_type=F32)
    gc = jnp.dot(h, wc_ref[...], preferred_element_type=F32)
    o_ref[...] = (u * _sigmoid(g)).astype(BF16)
    sgc_ref[...] = _silu(gc).astype(BF16)
    wcast_ref[...] = wsrc_ref[...].astype(BF16)


def _glu_proj(h, w_conv_bf, w_in, width):
    T, D = h.shape
    tn = GLU_TN
    gt = width // tn
    rows, cols = T // PROJ_TM, gt
    cast_cols = 4 * width // tn
    assert (D // CAST_ROWS) * cast_cols == rows * cols
    cast_map = lambda i, j: ((i * cols + j) // cast_cols, (i * cols + j) % cast_cols)
    return pl.pallas_call(
        _glu_kernel,
        grid=(rows, cols),
        in_specs=[pl.BlockSpec((PROJ_TM, D), lambda i, j: (i, 0)),
                  pl.BlockSpec((D, tn), lambda i, j: (0, j)),
                  pl.BlockSpec((D, tn), lambda i, j: (0, gt + j)),
                  pl.BlockSpec((D, tn), lambda i, j: (0, 2 * gt + j)),
                  pl.BlockSpec((CAST_ROWS, tn), cast_map)],
        out_specs=[pl.BlockSpec((PROJ_TM, tn), lambda i, j: (i, j)),
                   pl.BlockSpec((PROJ_TM, tn), lambda i, j: (i, j)),
                   pl.BlockSpec((CAST_ROWS, tn), cast_map)],
        out_shape=[jax.ShapeDtypeStruct((T, width), BF16), jax.ShapeDtypeStruct((T, width), BF16),
                   jax.ShapeDtypeStruct((D, 4 * width), BF16)],
        compiler_params=_params("arbitrary", "arbitrary"),
        name="glu_proj",
    )(h, w_conv_bf, w_conv_bf, w_conv_bf, w_in)


def _attn_kernel(q_ref, k_ref, v_ref, sg_ref, tri_ref, o_ref, acc_ref, r_ref, z_ref, c_ref, w_ref):
    tq, tk = ATTN_TQ, ATTN_TK
    sub = tq // tk
    tri = tri_ref[...]

    def cost(z):
        return jnp.where(z > SOFTPLUS_LINEAR_ABOVE, z, jnp.log(1.0 + jnp.exp2(z)) * LOG2E)

    def causal(shape):
        return lax.broadcasted_iota(jnp.int32, shape, 1) < lax.broadcasted_iota(jnp.int32, shape, 0)

    def query_tile(qi):
        q0 = qi * tq

        def stage_scores(tile, slot):
            row0, k0, masked = tile
            rows = slice(row0, tq)
            z = lax.dot_general(q_ref[q0 + row0:q0 + tq, :], k_ref[pl.ds(k0, tk), :], (((1,), (1,)), ((), ())),
                                preferred_element_type=F32)
            c = cost(z)
            if masked:
                c = jnp.where(causal(z.shape), c, 0.0)
            z_ref[slot, rows, :] = z
            c_ref[slot, rows, :] = c.astype(BF16)

        def stage_weights(tile, slot):
            row0, _, masked = tile
            rows = slice(row0, tq)
            cs = jnp.dot(c_ref[slot, rows, :], tri, preferred_element_type=F32)
            r = r_ref[rows, :]
            w = jnp.exp2(z_ref[slot, rows, :] - cs - r)
            if masked:
                w = jnp.where(causal(w.shape), w, 0.0)
            w_ref[slot, rows, :] = w.astype(BF16)
            r_ref[rows, :] = r + cs[:, 0:1]

        def stage_values(tile, slot):
            row0, k0, _ = tile
            rows = slice(row0, tq)
            acc_ref[rows, :] += jnp.dot(w_ref[slot, rows, :], v_ref[pl.ds(k0, tk), :], preferred_element_type=F32)

        def pipeline_steps(tiles, steps):
            for s in steps:
                if 0 <= s < len(tiles):
                    stage_scores(tiles[s], s % 2)
                if 0 <= s - 1 < len(tiles):
                    stage_weights(tiles[s - 1], (s - 1) % 2)
                if 0 <= s - 2 < len(tiles):
                    stage_values(tiles[s - 2], (s - 2) % 2)

        acc_ref[...] = jnp.zeros_like(acc_ref)
        r_ref[...] = jnp.zeros_like(r_ref)
        diag = [(d * tk, q0 + d * tk, True) for d in reversed(range(sub))]
        n = qi * sub
        early = lambda j: (0, (n - 1 - j) * tk, False)
        if n == 0:
            pipeline_steps(diag, range(sub + 2))
        else:
            pipeline_steps(diag + [early(0), early(1)], range(sub + 2))

            def body(p, carry):
                j = 2 * p
                dyn = lambda jj: (0, pl.multiple_of((n - 1 - jj) * tk, tk), False)
                stage_scores(dyn(j + 2), 0)
                stage_weights(dyn(j + 1), 1)
                stage_values(dyn(j), 0)
                stage_scores(dyn(j + 3), 1)
                stage_weights(dyn(j + 2), 0)
                stage_values(dyn(j + 1), 1)
                return carry

            lax.fori_loop(0, n // 2 - 1, body, 0)
            stage_weights(early(n - 1), 1)
            stage_values(early(n - 2), 0)
            stage_values(early(n - 1), 1)

        o_ref[q0:q0 + tq, :] = (acc_ref[...] * sg_ref[q0:q0 + tq, :].astype(F32)).astype(BF16)

    for qi in range(q_ref.shape[0] // tq):
        query_tile(qi)


def _sb_attention(p3, width):
    B, S, _ = p3.shape
    H = width // HEAD_DIM
    tq, tk = ATTN_TQ, ATTN_TK
    assert S % tq == 0
    idx = jnp.arange(tk, dtype=jnp.int32)
    tri = (idx[:, None] >= idx[None, :]).astype(BF16)
    head_block = lambda group: pl.BlockSpec((None, S, HEAD_DIM), lambda b, h: (b, 0, group * H + h))
    return pl.pallas_call(
        _attn_kernel,
        grid=(B, H),
        in_specs=[head_block(0), head_block(1), head_block(2), head_block(3),
                  pl.BlockSpec(tri.shape, lambda b, h: (0, 0))],
        out_specs=head_block(0),
        out_shape=jax.ShapeDtypeStruct((B, S, width), BF16),
        scratch_shapes=[pltpu.VMEM((tq, HEAD_DIM), F32), pltpu.VMEM((tq, 1), F32),
                        pltpu.VMEM((2, tq, tk), F32), pltpu.VMEM((2, tq, tk), BF16),
                        pltpu.VMEM((2, tq, tk), BF16)],
        compiler_params=_params("arbitrary", "arbitrary"),
        name="sb_attn",
    )(p3, p3, p3, p3, tri)


def _conv_kernel(cv_ref, sgc_ref, lng_ref, lnb_ref, wpw_ref, bpw_ref, o_ref):
    cv = cv_ref[...]
    mu = jnp.mean(cv, axis=-1, keepdims=True)
    xc = cv - mu
    var = jnp.mean(xc * xc, axis=-1, keepdims=True)
    yn = xc * lax.rsqrt(var + EPS) * lng_ref[...] + lnb_ref[...]
    y = jnp.dot(_silu(yn).astype(BF16), wpw_ref[...], preferred_element_type=F32) + bpw_ref[...]
    o_ref[...] = (y * sgc_ref[...].astype(F32)).astype(BF16)


def _conv_mix(cv, sgc, ln_g, ln_b, w_pw_bf, b_pw):
    T, C = cv.shape
    ts = CONV_TS
    vec = lambda a: a.reshape(1, C)
    const2 = lambda s: (0, 0)
    return pl.pallas_call(
        _conv_kernel,
        grid=(T // ts,),
        in_specs=[pl.BlockSpec((ts, C), lambda s: (s, 0)),
                  pl.BlockSpec((ts, C), lambda s: (s, 0)),
                  pl.BlockSpec((1, C), const2), pl.BlockSpec((1, C), const2),
                  pl.BlockSpec((C, C), const2), pl.BlockSpec((1, C), const2)],
        out_specs=pl.BlockSpec((ts, C), lambda s: (s, 0)),
        out_shape=jax.ShapeDtypeStruct((T, C), BF16),
        compiler_params=_params("arbitrary"),
        name="conv_mix",
    )(cv, sgc, vec(ln_g), vec(ln_b), w_pw_bf, vec(b_pw))


def _outproj_kernel(ysb_ref, ycv_ref, wt_ref, wb_ref, x_ref, gate_ref, o_ref):
    y = (jnp.dot(ysb_ref[...], wt_ref[...], preferred_element_type=F32)
         + jnp.dot(ycv_ref[...], wb_ref[...], preferred_element_type=F32))
    o_ref[...] = x_ref[...] + gate_ref[0] * y


def _out_proj(y_sb, y_conv, w_out_bf, x2, mod3, seq):
    T, W = y_sb.shape
    D = x2.shape[1]
    tiles_per_batch = seq // OUT_TM
    return pl.pallas_call(
        _outproj_kernel,
        grid=(T // OUT_TM, D // OUT_TN),
        in_specs=[pl.BlockSpec((OUT_TM, W), lambda i, j: (i, 0)),
                  pl.BlockSpec((OUT_TM, W), lambda i, j: (i, 0)),
                  pl.BlockSpec((W, OUT_TN), lambda i, j: (0, j)),
                  pl.BlockSpec((W, OUT_TN), lambda i, j: (1, j)),
                  pl.BlockSpec((OUT_TM, OUT_TN), lambda i, j: (i, j)),
                  pl.BlockSpec((1, 1, OUT_TN), lambda i, j: (3 * (i // tiles_per_batch) + 2, 0, j))],
        out_specs=pl.BlockSpec((OUT_TM, OUT_TN), lambda i, j: (i, j)),
        out_shape=jax.ShapeDtypeStruct((T, D), F32),
        compiler_params=_params("arbitrary", "arbitrary"),
        name="out_proj",
    )(y_sb, y_conv, w_out_bf, w_out_bf, x2, mod3)


def _layer(x, c, norm_g, w_ada, b_ada, w_in, q_norm_g, k_norm_g, w_dw, b_dw, ln_g, ln_b, w_pw, b_pw, w_out):
    B, S, D = x.shape
    width = w_pw.shape[0]
    assert w_in.shape[1] == 7 * width and w_out.shape[0] == 2 * width
    x2 = x.reshape(B * S, D)

    mod = _adaln_mod(c, w_ada, b_ada)
    mod3 = mod.reshape(B * 3, 1, D)
    h = _norm_mod(x2, norm_g, mod3, S)

    hglu, sgc, w_attn_bf = _glu_proj(h, w_in[:, 4 * width:].astype(BF16), w_in, width)
    p2, cv, w_out_bf, w_pw_bf = _in_proj(h, w_attn_bf, q_norm_g, k_norm_g, hglu, w_dw, b_dw, w_out, w_pw, width, S)

    y_sb = _sb_attention(p2.reshape(B, S, 4 * width), width).reshape(B * S, width)
    y_conv = _conv_mix(cv, sgc, ln_g, ln_b, w_pw_bf, b_pw)

    out = _out_proj(y_sb, y_conv, w_out_bf, x2, mod3, S)
    return out.reshape(B, S, D)


def kernel(x, c, norm_g, w_ada, b_ada, w_in, q_norm_g, k_norm_g, w_dw, b_dw, ln_g, ln_b, w_pw, b_pw, w_out):
    for layer in range(norm_g.shape[0]):
        x = _layer(x, c, norm_g[layer], w_ada[layer], b_ada[layer], w_in[layer], q_norm_g[layer],
                   k_norm_g[layer], w_dw[layer], b_dw[layer], ln_g[layer], ln_b[layer], w_pw[layer],
                   b_pw[layer], w_out[layer])
    return x
```

```python
import functools
import math

import jax
import jax.numpy as jnp
from jax import lax
from jax.experimental import pallas as pl
from jax.experimental.pallas import tpu as pltpu

F32 = jnp.float32
BF16 = jnp.bfloat16

HEAD_DIM = 128
CONV_KERNEL = 31
EPS = 1e-6
LOG2E = math.log2(math.e)

LANES = 128
SUBLANES = 8
HALO_ROWS = 32
VMEM_LIMIT = 56 * 1024 * 1024

NORM_TM = 512
PROJ_TM, PROJ_TN = 1024, 512
GLU_TN = 256
CAST_ROWS = 1024
CAST_COLS = 256
ATTN_TQ, ATTN_TK = 1024, 256
assert (ATTN_TQ // ATTN_TK) % 2 == 0
SOFTPLUS_LINEAR_ABOVE = 64.0
CONV_TS = 512
CONV_CHAINS = 2
OUT_TM, OUT_TN = 1024, 1024
ADA_TN = 512


def _params(*sem):
    return pltpu.CompilerParams(dimension_semantics=sem, vmem_limit_bytes=VMEM_LIMIT)


def _sigmoid(x):
    return 0.5 * jnp.tanh(0.5 * x) + 0.5


def _silu(x):
    return x * _sigmoid(x)


def _adaln_kernel(c_ref, w_ref, b_ref, o_ref):
    a = _silu(c_ref[...]).astype(BF16)
    o_ref[...] = jnp.dot(a, w_ref[...].astype(BF16), preferred_element_type=F32) + b_ref[...]


def _adaln_mod(c, w_ada, b_ada):
    B, D = c.shape
    N = w_ada.shape[1]
    return pl.pallas_call(
        _adaln_kernel,
        grid=(N // ADA_TN,),
        in_specs=[pl.BlockSpec((B, D), lambda j: (0, 0)),
                  pl.BlockSpec((D, ADA_TN), lambda j: (0, j)),
                  pl.BlockSpec((1, ADA_TN), lambda j: (0, j))],
        out_specs=pl.BlockSpec((B, ADA_TN), lambda j: (0, j)),
        out_shape=jax.ShapeDtypeStruct((B, N), F32),
        compiler_params=_params("arbitrary"),
        name="adaln_mod",
    )(c, w_ada, b_ada.reshape(1, N))


def _norm_kernel(x_ref, g_ref, shift_ref, scale_ref, o_ref):
    x = x_ref[...]
    ms = jnp.mean(x * x, axis=-1, keepdims=True)
    y = x * lax.rsqrt(ms + EPS) * g_ref[...]
    o_ref[...] = (y * (1.0 + scale_ref[0]) + shift_ref[0]).astype(BF16)


def _norm_mod(x2, norm_g, mod3, seq):
    T, D = x2.shape
    tiles_per_batch = seq // NORM_TM
    return pl.pallas_call(
        _norm_kernel,
        grid=(T // NORM_TM,),
        in_specs=[pl.BlockSpec((NORM_TM, D), lambda i: (i, 0)),
                  pl.BlockSpec((1, D), lambda i: (0, 0)),
                  pl.BlockSpec((1, 1, D), lambda i: (3 * (i // tiles_per_batch), 0, 0)),
                  pl.BlockSpec((1, 1, D), lambda i: (3 * (i // tiles_per_batch) + 1, 0, 0))],
        out_specs=pl.BlockSpec((NORM_TM, D), lambda i: (i, 0)),
        out_shape=jax.ShapeDtypeStruct((T, D), BF16),
        compiler_params=_params("arbitrary"),
        name="norm_mod",
    )(x2, norm_g.reshape(1, D), mod3, mod3)


def _head_rms(a, g):
    parts = []
    for hh in range(a.shape[1] // HEAD_DIM):
        c = a[:, hh * HEAD_DIM:(hh + 1) * HEAD_DIM]
        ms = jnp.mean(c * c, axis=-1, keepdims=True)
        parts.append(c * lax.rsqrt(ms + EPS))
    return jnp.concatenate(parts, axis=-1) * g


def _depthwise_chunk(hg_ref, halo_ref, wdw_ref, bdw_ref, cv_ref, hp_ref, sequence_start, never):
    rows = hg_ref.shape[0]
    first_tap_row = HALO_ROWS - (CONV_KERNEL - 1)
    hp_ref[0:HALO_ROWS, :] = jnp.where(sequence_start, 0.0, halo_ref[...].astype(F32))
    hp_ref[HALO_ROWS:, :] = hg_ref[...].astype(F32)
    bias = bdw_ref[0]
    taps = [wdw_ref[0, k] for k in range(CONV_KERNEL)]
    prev = [bias] * CONV_CHAINS
    for g in range(rows // SUBLANES):
        acc = jnp.where(never, prev[g % CONV_CHAINS], bias)
        for k in range(CONV_KERNEL):
            start = SUBLANES * g + first_tap_row + k
            acc = acc + hp_ref[start:start + SUBLANES, :] * taps[k]
        cv_ref[SUBLANES * g:SUBLANES * (g + 1), :] = acc
        prev[g % CONV_CHAINS] = acc


def _inproj_kernel(h_ref, w_ref, qg_ref, kg_ref, hg_ref, halo_ref, wdw_ref, bdw_ref, wo_ref, wp_ref,
                   o_ref, cv_ref, wo_cast_ref, wp_cast_ref, hp_ref, *, group_tiles, q_scale, tiles_per_seq):
    i = pl.program_id(0)
    j = pl.program_id(1)
    wo_cast_ref[...] = wo_ref[...].astype(BF16)
    wp_cast_ref[...] = wp_ref[...].astype(BF16)
    acc = jnp.dot(h_ref[...], w_ref[...], preferred_element_type=F32)
    _depthwise_chunk(hg_ref, halo_ref, wdw_ref, bdw_ref, cv_ref, hp_ref, i % tiles_per_seq == 0, i < 0)

    @pl.when(j < group_tiles)
    def _():
        o_ref[...] = (_head_rms(acc, qg_ref[...]) * q_scale).astype(BF16)

    @pl.when((j >= group_tiles) & (j < 2 * group_tiles))
    def _():
        o_ref[...] = _head_rms(acc, kg_ref[...]).astype(BF16)

    @pl.when((j >= 2 * group_tiles) & (j < 3 * group_tiles))
    def _():
        o_ref[...] = acc.astype(BF16)

    @pl.when(j >= 3 * group_tiles)
    def _():
        o_ref[...] = _silu(acc).astype(BF16)


def _in_proj(h, w_bf, q_norm_g, k_norm_g, hglu, w_dw, b_dw, w_out, w_pw, width, seq):
    T, D = h.shape
    gt = width // PROJ_TN
    heads_per_tile = PROJ_TN // HEAD_DIM
    qg = jnp.tile(q_norm_g.reshape(1, HEAD_DIM), (1, heads_per_tile))
    kg = jnp.tile(k_norm_g.reshape(1, HEAD_DIM), (1, heads_per_tile))
    q_scale = (HEAD_DIM ** -0.5) * LOG2E
    chunks = width // LANES
    assert chunks == 4 * gt and seq % PROJ_TM == 0
    halo_per_tile = PROJ_TM // HALO_ROWS
    wdw_b = jnp.broadcast_to(w_dw.reshape(CONV_KERNEL, chunks, 1, LANES).transpose(1, 0, 2, 3),
                             (chunks, CONV_KERNEL, SUBLANES, LANES))
    bdw_b = jnp.broadcast_to(b_dw.reshape(chunks, 1, LANES), (chunks, SUBLANES, LANES))

    steps = (T // PROJ_TM) * 4 * gt

    def cast_spec(w):
        col_blocks = w.shape[1] // CAST_COLS
        block_rows = w.shape[0] * col_blocks // steps
        assert block_rows % (2 * SUBLANES) == 0 and block_rows * steps == w.shape[0] * col_blocks
        return pl.BlockSpec((block_rows, CAST_COLS),
                            lambda i, j: ((i * (4 * gt) + j) // col_blocks, (i * (4 * gt) + j) % col_blocks))

    return pl.pallas_call(
        functools.partial(_inproj_kernel, group_tiles=gt, q_scale=q_scale, tiles_per_seq=seq // PROJ_TM),
        grid=(T // PROJ_TM, 4 * gt),
        in_specs=[pl.BlockSpec((PROJ_TM, D), lambda i, j: (i, 0)),
                  pl.BlockSpec((D, PROJ_TN), lambda i, j: (0, j)),
                  pl.BlockSpec((1, PROJ_TN), lambda i, j: (0, 0)),
                  pl.BlockSpec((1, PROJ_TN), lambda i, j: (0, 0)),
                  pl.BlockSpec((PROJ_TM, LANES), lambda i, j: (i, j)),
                  pl.BlockSpec((HALO_ROWS, LANES), lambda i, j: (jnp.maximum(i * halo_per_tile - 1, 0), j)),
                  pl.BlockSpec((1, CONV_KERNEL, SUBLANES, LANES), lambda i, j: (j, 0, 0, 0)),
                  pl.BlockSpec((1, SUBLANES, LANES), lambda i, j: (j, 0, 0)),
                  cast_spec(w_out),
                  cast_spec(w_pw)],
        out_specs=[pl.BlockSpec((PROJ_TM, PROJ_TN), lambda i, j: (i, j)),
                   pl.BlockSpec((PROJ_TM, LANES), lambda i, j: (i, j)),
                   cast_spec(w_out),
                   cast_spec(w_pw)],
        out_shape=[jax.ShapeDtypeStruct((T, 4 * width), BF16), jax.ShapeDtypeStruct((T, width), F32),
                   jax.ShapeDtypeStruct(w_out.shape, BF16), jax.ShapeDtypeStruct(w_pw.shape, BF16)],
        scratch_shapes=[pltpu.VMEM((HALO_ROWS + PROJ_TM, LANES), F32)],
        compiler_params=_params("arbitrary", "arbitrary"),
        name="in_proj",
    )(h, w_bf, qg, kg, hglu, hglu, wdw_b, bdw_b, w_out, w_pw)


def _glu_kernel(h_ref, wu_ref, wg_ref, wc_ref, wsrc_ref, o_ref, sgc_ref, wcast_ref):
    h = h_ref[...]
    u = jnp.dot(h, wu_ref[...], preferred_element_type=F32)
    g = jnp.dot(h, wg_ref[...], preferred_element_type=F32)
    gc = jnp.dot(h, wc_ref[...], preferred_element_type=F32)
    o_ref[...] = (u * _sigmoid(g)).astype(BF16)
    sgc_ref[...] = _silu(gc).astype(BF16)
    wcast_ref[...] = wsrc_ref[...].astype(BF16)


def _glu_proj(h, w_conv_bf, w_in, width):
    T, D = h.shape
    tn = GLU_TN
    gt = width // tn
    rows, cols = T // PROJ_TM, gt
    cast_cols = 4 * width // tn
    assert (D // CAST_ROWS) * cast_cols == rows * cols
    cast_map = lambda i, j: ((i * cols + j) // cast_cols, (i * cols + j) % cast_cols)
    return pl.pallas_call(
        _glu_kernel,
        grid=(rows, cols),
        in_specs=[pl.BlockSpec((PROJ_TM, D), lambda i, j: (i, 0)),
                  pl.BlockSpec((D, tn), lambda i, j: (0, j)),
                  pl.BlockSpec((D, tn), lambda i, j: (0, gt + j)),
                  pl.BlockSpec((D, tn), lambda i, j: (0, 2 * gt + j)),
                  pl.BlockSpec((CAST_ROWS, tn), cast_map)],
        out_specs=[pl.BlockSpec((PROJ_TM, tn), lambda i, j: (i, j)),
                   pl.BlockSpec((PROJ_TM, tn), lambda i, j: (i, j)),
                   pl.BlockSpec((CAST_ROWS, tn), cast_map)],
        out_shape=[jax.ShapeDtypeStruct((T, width), BF16), jax.ShapeDtypeStruct((T, width), BF16),
                   jax.ShapeDtypeStruct((D, 4 * width), BF16)],
        compiler_params=_params("arbitrary", "arbitrary"),
        name="glu_proj",
    )(h, w_conv_bf, w_conv_bf, w_conv_bf, w_in)


def _attn_kernel(q_ref, k_ref, v_ref, sg_ref, tri_ref, o_ref, acc_ref, r_ref, z_ref, c_ref, w_ref):
    tq, tk = ATTN_TQ, ATTN_TK
    sub = tq // tk
    tri = tri_ref[...]

    def cost(z):
        return jnp.where(z > SOFTPLUS_LINEAR_ABOVE, z, jnp.log(1.0 + jnp.exp2(z)) * LOG2E)

    def causal(shape):
        return lax.broadcasted_iota(jnp.int32, shape, 1) < lax.broadcasted_iota(jnp.int32, shape, 0)

    def query_tile(qi):
        q0 = qi * tq

        def stage_scores(tile, slot):
            row0, k0, masked = tile
            rows = slice(row0, tq)
            z = lax.dot_general(q_ref[q0 + row0:q0 + tq, :], k_ref[pl.ds(k0, tk), :], (((1,), (1,)), ((), ())),
                                preferred_element_type=F32)
            c = cost(z)
            if masked:
                c = jnp.where(causal(z.shape), c, 0.0)
            z_ref[slot, rows, :] = z
            c_ref[slot, rows, :] = c.astype(BF16)

        def stage_weights(tile, slot):
            row0, _, masked = tile
            rows = slice(row0, tq)
            cs = jnp.dot(c_ref[slot, rows, :], tri, preferred_element_type=F32)
            r = r_ref[rows, :]
            w = jnp.exp2(z_ref[slot, rows, :] - cs - r)
            if masked:
                w = jnp.where(causal(w.shape), w, 0.0)
            w_ref[slot, rows, :] = w.astype(BF16)
            r_ref[rows, :] = r + cs[:, 0:1]

        def stage_values(tile, slot):
            row0, k0, _ = tile
            rows = slice(row0, tq)
            acc_ref[rows, :] += jnp.dot(w_ref[slot, rows, :], v_ref[pl.ds(k0, tk), :], preferred_element_type=F32)

        def pipeline_steps(tiles, steps):
            for s in steps:
                if 0 <= s < len(tiles):
                    stage_scores(tiles[s], s % 2)
                if 0 <= s - 1 < len(tiles):
                    stage_weights(tiles[s - 1], (s - 1) % 2)
                if 0 <= s - 2 < len(tiles):
                    stage_values(tiles[s - 2], (s - 2) % 2)

        acc_ref[...] = jnp.zeros_like(acc_ref)
        r_ref[...] = jnp.zeros_like(r_ref)
        diag = [(d * tk, q0 + d * tk, True) for d in reversed(range(sub))]
        n = qi * sub
        early = lambda j: (0, (n - 1 - j) * tk, False)
        if n == 0:
            pipeline_steps(diag, range(sub + 2))
        else:
            pipeline_steps(diag + [early(0), early(1)], range(sub + 2))

            for j in range(0, n - 2, 2):
                stage_scores(early(j + 2), 0)
                stage_weights(early(j + 1), 1)
                stage_values(early(j), 0)
                stage_scores(early(j + 3), 1)
                stage_weights(early(j + 2), 0)
                stage_values(early(j + 1), 1)
            stage_weights(early(n - 1), 1)
            stage_values(early(n - 2), 0)
            stage_values(early(n - 1), 1)

        o_ref[q0:q0 + tq, :] = (acc_ref[...] * sg_ref[q0:q0 + tq, :].astype(F32)).astype(BF16)

    for qi in range(q_ref.shape[0] // tq):
        query_tile(qi)


def _sb_attention(p3, width):
    B, S, _ = p3.shape
    H = width // HEAD_DIM
    tq, tk = ATTN_TQ, ATTN_TK
    assert S % tq == 0
    idx = jnp.arange(tk, dtype=jnp.int32)
    tri = (idx[:, None] >= idx[None, :]).astype(BF16)
    head_block = lambda group: pl.BlockSpec((None, S, HEAD_DIM), lambda b, h: (b, 0, group * H + h))
    return pl.pallas_call(
        _attn_kernel,
        grid=(B, H),
        in_specs=[head_block(0), head_block(1), head_block(2), head_block(3),
                  pl.BlockSpec(tri.shape, lambda b, h: (0, 0))],
        out_specs=head_block(0),
        out_shape=jax.ShapeDtypeStruct((B, S, width), BF16),
        scratch_shapes=[pltpu.VMEM((tq, HEAD_DIM), F32), pltpu.VMEM((tq, 1), F32),
                        pltpu.VMEM((2, tq, tk), F32), pltpu.VMEM((2, tq, tk), BF16),
                        pltpu.VMEM((2, tq, tk), BF16)],
        compiler_params=_params("arbitrary", "arbitrary"),
        name="sb_attn",
    )(p3, p3, p3, p3, tri)


def _conv_kernel(cv_ref, sgc_ref, lng_ref, lnb_ref, wpw_ref, bpw_ref, o_ref):
    cv = cv_ref[...]
    mu = jnp.mean(cv, axis=-1, keepdims=True)
    xc = cv - mu
    var = jnp.mean(xc * xc, axis=-1, keepdims=True)
    yn = xc * lax.rsqrt(var + EPS) * lng_ref[...] + lnb_ref[...]
    y = jnp.dot(_silu(yn).astype(BF16), wpw_ref[...], preferred_element_type=F32) + bpw_ref[...]
    o_ref[...] = (y * sgc_ref[...].astype(F32)).astype(BF16)


def _conv_mix(cv, sgc, ln_g, ln_b, w_pw_bf, b_pw):
    T, C = cv.shape
    ts = CONV_TS
    vec = lambda a: a.reshape(1, C)
    const2 = lambda s: (0, 0)
    return pl.pallas_call(
        _conv_kernel,
        grid=(T // ts,),
        in_specs=[pl.BlockSpec((ts, C), lambda s: (s, 0)),
                  pl.BlockSpec((ts, C), lambda s: (s, 0)),
                  pl.BlockSpec((1, C), const2), pl.BlockSpec((1, C), const2),
                  pl.BlockSpec((C, C), const2), pl.BlockSpec((1, C), const2)],
        out_specs=pl.BlockSpec((ts, C), lambda s: (s, 0)),
        out_shape=jax.ShapeDtypeStruct((T, C), BF16),
        compiler_params=_params("arbitrary"),
        name="conv_mix",
    )(cv, sgc, vec(ln_g), vec(ln_b), w_pw_bf, vec(b_pw))


def _outproj_kernel(ysb_ref, ycv_ref, wt_ref, wb_ref, x_ref, gate_ref, o_ref):
    y = (jnp.dot(ysb_ref[...], wt_ref[...], preferred_element_type=F32)
         + jnp.dot(ycv_ref[...], wb_ref[...], preferred_element_type=F32))
    o_ref[...] = x_ref[...] + gate_ref[0] * y


def _out_proj(y_sb, y_conv, w_out_bf, x2, mod3, seq):
    T, W = y_sb.shape
    D = x2.shape[1]
    tiles_per_batch = seq // OUT_TM
    return pl.pallas_call(
        _outproj_kernel,
        grid=(T // OUT_TM, D // OUT_TN),
        in_specs=[pl.BlockSpec((OUT_TM, W), lambda i, j: (i, 0)),
                  pl.BlockSpec((OUT_TM, W), lambda i, j: (i, 0)),
                  pl.BlockSpec((W, OUT_TN), lambda i, j: (0, j)),
                  pl.BlockSpec((W, OUT_TN), lambda i, j: (1, j)),
                  pl.BlockSpec((OUT_TM, OUT_TN), lambda i, j: (i, j)),
                  pl.BlockSpec((1, 1, OUT_TN), lambda i, j: (3 * (i // tiles_per_batch) + 2, 0, j))],
        out_specs=pl.BlockSpec((OUT_TM, OUT_TN), lambda i, j: (i, j)),
        out_shape=jax.ShapeDtypeStruct((T, D), F32),
        compiler_params=_params("arbitrary", "arbitrary"),
        name="out_proj",
    )(y_sb, y_conv, w_out_bf, w_out_bf, x2, mod3)


def _layer(x, c, norm_g, w_ada, b_ada, w_in, q_norm_g, k_norm_g, w_dw, b_dw, ln_g, ln_b, w_pw, b_pw, w_out):
    B, S, D = x.shape
    width = w_pw.shape[0]
    assert w_in.shape[1] == 7 * width and w_out.shape[0] == 2 * width
    x2 = x.reshape(B * S, D)

    mod = _adaln_mod(c, w_ada, b_ada)
    mod3 = mod.reshape(B * 3, 1, D)
    h = _norm_mod(x2, norm_g, mod3, S)

    hglu, sgc, w_attn_bf = _glu_proj(h, w_in[:, 4 * width:].astype(BF16), w_in, width)
    p2, cv, w_out_bf, w_pw_bf = _in_proj(h, w_attn_bf, q_norm_g, k_norm_g, hglu, w_dw, b_dw, w_out, w_pw, width, S)

    y_sb = _sb_attention(p2.reshape(B, S, 4 * width), width).reshape(B * S, width)
    y_conv = _conv_mix(cv, sgc, ln_g, ln_b, w_pw_bf, b_pw)

    out = _out_proj(y_sb, y_conv, w_out_bf, x2, mod3, S)
    return out.reshape(B, S, D)


def kernel(x, c, norm_g, w_ada, b_ada, w_in, q_norm_g, k_norm_g, w_dw, b_dw, ln_g, ln_b, w_pw, b_pw, w_out):
    for layer in range(norm_g.shape[0]):
        x = _layer(x, c, norm_g[layer], w_ada[layer], b_ada[layer], w_in[layer], q_norm_g[layer],
                   k_norm_g[layer], w_dw[layer], b_dw[layer], ln_g[layer], ln_b[layer], w_pw[layer],
                   b_pw[layer], w_out[layer])
    return x
```
